```python
import numpy as np
import jax
import jax.numpy as jnp
from jax import lax

D_MODEL = 1024
BATCH = 32
SEQ = 2048
DEPTH = 1
DEC_BATCH = 8
DEC_SEQ = 2048
PAST_LEN = 128

HEAD_DIM = 64
A_HEADS = 8
A_KV_HEADS = 2
A_GROUP = A_HEADS // A_KV_HEADS
B_HEADS = 8
A_WIDTH = A_HEADS * HEAD_DIM
A_KV_WIDTH = A_KV_HEADS * HEAD_DIM
B_WIDTH = B_HEADS * HEAD_DIM
MIX_WIDTH = A_WIDTH + B_WIDTH
WINDOW = 128
A_BLOCK = 128
ROT_DIM = HEAD_DIM // 4
ROPE_THETA = 500000.0
GRID_W = 64
NA_ROWS_MAX = 8
NA_COLS = 16
NA_COL_BLOCK = 16
NA_KEY_COLS = 32
NA_N_COL_BLOCKS = GRID_W // NA_COL_BLOCK
PLE_DIM = 256
EPS = 1e-6
IN_WIDTHS = (A_WIDTH, A_KV_WIDTH, A_KV_WIDTH, A_WIDTH, B_WIDTH, B_WIDTH, B_WIDTH, B_WIDTH)
IN_WIDTH = sum(IN_WIDTHS)
IN_SPLITS = [int(v) for v in np.cumsum(IN_WIDTHS)[:-1]]

kernel_name = 'hymba_style_window_gqa_neighbourhood_encoder'


def rms_norm(x, g):
    xf = x.astype(jnp.float32)
    y = xf * lax.rsqrt(jnp.mean(xf * xf, axis=-1, keepdims=True) + EPS)
    return (y * g.astype(jnp.float32)).astype(x.dtype)


def partial_rope(x, positions):
    inv_freq = jnp.power(ROPE_THETA, -jnp.arange(0, ROT_DIM, 2, dtype=jnp.float32) / ROT_DIM)
    ang = positions.astype(jnp.float32)[:, None] * inv_freq[None, :]
    cos = jnp.cos(ang)[:, None, :]
    sin = jnp.sin(ang)[:, None, :]
    xr = x[..., :ROT_DIM].astype(jnp.float32)
    x1, x2 = xr[..., :ROT_DIM // 2], xr[..., ROT_DIM // 2:]
    rot = jnp.concatenate([x1 * cos - x2 * sin, x2 * cos + x1 * sin], axis=-1)
    return jnp.concatenate([rot.astype(x.dtype), x[..., ROT_DIM:]], axis=-1)


def windowed_gqa_sink(q, k, v, sink):
    b, s = q.shape[0], q.shape[1]
    nb = s // A_BLOCK
    span = A_BLOCK + 2 * WINDOW
    pad = ((0, 0), (WINDOW, WINDOW), (0, 0), (0, 0))
    kp = jnp.pad(k, pad)
    vp = jnp.pad(v, pad)
    qi = np.arange(A_BLOCK)[:, None]
    si = np.arange(span)[None, :]
    band = np.abs(si - WINDOW - qi) <= WINDOW
    sink_f = sink.astype(jnp.float32).reshape(A_KV_HEADS, A_GROUP)[None, :, :, None, None]
    scale = HEAD_DIM ** -0.5

    def block(j):
        start = j * A_BLOCK
        qb = lax.dynamic_slice_in_dim(q, start, A_BLOCK, axis=1)
        qb = qb.reshape(b, A_BLOCK, A_KV_HEADS, A_GROUP, HEAD_DIM)
        kb = lax.dynamic_slice_in_dim(kp, start, span, axis=1)
        vb = lax.dynamic_slice_in_dim(vp, start, span, axis=1)
        sc = jnp.einsum('bqkgd,bskd->bkgqs', qb, kb).astype(jnp.float32) * scale
        kpos = start - WINDOW + jnp.arange(span)
        valid = band & ((kpos >= 0) & (kpos < s))[None, :]
        sc = jnp.where(valid, sc, -jnp.inf)
        m = jnp.maximum(jnp.max(sc, axis=-1, keepdims=True), sink_f)
        e = jnp.exp(sc - m)
        denom = jnp.sum(e, axis=-1, keepdims=True) + jnp.exp(sink_f - m)
        pr = (e / denom).astype(vb.dtype)
        o = jnp.einsum('bkgqs,bskd->bqkgd', pr, vb)
        return o.reshape(b, A_BLOCK, A_WIDTH)

    out = lax.map(block, jnp.arange(nb))
    return out.transpose(1, 0, 2, 3).reshape(b, s, A_WIDTH)


def neighbourhood_attn(q, k, v, rpb):
    b, s = q.shape[0], q.shape[1]
    rows = s // GRID_W
    wr = min(NA_ROWS_MAX, rows)
    qg = q.reshape(b, rows, GRID_W, B_HEADS, HEAD_DIM)
    kg = k.reshape(b, rows, GRID_W, B_HEADS, HEAD_DIM)
    vg = v.reshape(b, rows, GRID_W, B_HEADS, HEAD_DIM)
    cols = np.arange(GRID_W)
    col_start = np.clip(cols - NA_COLS // 2, 0, GRID_W - NA_COLS)
    blk_start = np.clip(np.arange(NA_N_COL_BLOCKS) * NA_COL_BLOCK - NA_COLS // 2,
                        0, GRID_W - NA_KEY_COLS)
    key_cols = blk_start[:, None] + np.arange(NA_KEY_COLS)[None, :]
    q_cols = cols.reshape(NA_N_COL_BLOCKS, NA_COL_BLOCK)
    qs = col_start.reshape(NA_N_COL_BLOCKS, NA_COL_BLOCK)[:, :, None]
    kc = key_cols[:, None, :]
    col_valid = (kc >= qs) & (kc < qs + NA_COLS)
    dcol = np.clip(kc - q_cols[:, :, None], -(NA_COLS - 1), NA_COLS - 1) + NA_COLS - 1
    rpb_cols = rpb[:, :, dcol]
    mask = col_valid[:, None, :, None, :]
    scale = HEAD_DIM ** -0.5

    def row(r):
        rs = jnp.clip(r - wr // 2, 0, rows - wr)
        qr = lax.dynamic_index_in_dim(qg, r, axis=1, keepdims=False)
        qr = qr.reshape(b, NA_N_COL_BLOCKS, NA_COL_BLOCK, B_HEADS, HEAD_DIM)
        kr = lax.dynamic_slice_in_dim(kg, rs, wr, axis=1)
        vr = lax.dynamic_slice_in_dim(vg, rs, wr, axis=1)
        kb = kr[:, :, key_cols]
        vb = vr[:, :, key_cols]
        sc = jnp.einsum('bnqhd,banchd->bnhqac', qr, kb).astype(jnp.float32) * scale
        drow = rs + jnp.arange(wr) - r + NA_ROWS_MAX - 1
        bias = rpb_cols[:, drow].transpose(2, 0, 3, 1, 4)
        sc = jnp.where(mask, sc + bias.astype(jnp.float32), -jnp.inf)
        sc = sc.reshape(b, NA_N_COL_BLOCKS, B_HEADS, NA_COL_BLOCK, wr * NA_KEY_COLS)
        pr = jax.nn.softmax(sc, axis=-1)
        pr = pr.reshape(b, NA_N_COL_BLOCKS, B_HEADS, NA_COL_BLOCK, wr, NA_KEY_COLS).astype(vb.dtype)
        o = jnp.einsum('bnhqac,banchd->bnqhd', pr, vb)
        return o.reshape(b, GRID_W, B_WIDTH)

    out = lax.map(row, jnp.arange(rows))
    return out.transpose(1, 0, 2, 3).reshape(b, s, B_WIDTH)


def encoder_layer(x, p, norm_w, w_in, q_norm_a, k_norm_a, sink_a,
                  q_norm_b, k_norm_b, rpb_b, w_out, w_ple, w_ple_gate):
    b, s, _ = x.shape
    h = rms_norm(x, norm_w)
    proj = h @ w_in
    q_a, k_a, v_a, g_a, q_b, k_b, v_b, g_b = jnp.split(proj, IN_SPLITS, axis=-1)
    pos = jnp.arange(s)
    q_a = partial_rope(rms_norm(q_a.reshape(b, s, A_HEADS, HEAD_DIM), q_norm_a), pos)
    k_a = partial_rope(rms_norm(k_a.reshape(b, s, A_KV_HEADS, HEAD_DIM), k_norm_a), pos)
    v_a = v_a.reshape(b, s, A_KV_HEADS, HEAD_DIM)
    o_a = windowed_gqa_sink(q_a, k_a, v_a, sink_a) * jax.nn.silu(g_a)
    q_b = rms_norm(q_b.reshape(b, s, B_HEADS, HEAD_DIM), q_norm_b)
    k_b = rms_norm(k_b.reshape(b, s, B_HEADS, HEAD_DIM), k_norm_b)
    v_b = v_b.reshape(b, s, B_HEADS, HEAD_DIM)
    o_b = neighbourhood_attn(q_b, k_b, v_b, rpb_b) * jax.nn.silu(g_b)
    x = x + jnp.concatenate([o_a, o_b], axis=-1) @ w_out
    gate = jax.nn.sigmoid(x @ w_ple_gate)
    return x + (p @ w_ple) * gate


def run_trunk(x, p, norm_w, w_in, q_norm_a, k_norm_a, sink_a,
              q_norm_b, k_norm_b, rpb_b, w_out, w_ple, w_ple_gate):
    for i in range(DEPTH):
        x = encoder_layer(x, p[i], norm_w[i], w_in[i], q_norm_a[i], k_norm_a[i], sink_a[i],
                          q_norm_b[i], k_norm_b[i], rpb_b[i], w_out[i], w_ple[i], w_ple_gate[i])
    return x


def setup_inputs(seed: int = 0) -> dict:
    key = jax.random.key(seed)
    ks = jax.random.split(key, 16)
    f32 = jnp.float32
    nrm = jax.random.normal
    return {
        'x_prompt': nrm(ks[0], (BATCH, SEQ, D_MODEL), f32),
        'x_sample': nrm(ks[1], (DEC_BATCH, DEC_SEQ, D_MODEL), f32),
        'p_prompt': nrm(ks[2], (DEPTH, BATCH, SEQ, PLE_DIM), f32),
        'p_sample': nrm(ks[3], (DEPTH, DEC_BATCH, DEC_SEQ, PLE_DIM), f32),
        'norm_w': 1.0 + 0.02 * nrm(ks[4], (DEPTH, D_MODEL), f32),
        'w_in': nrm(ks[5], (DEPTH, D_MODEL, IN_WIDTH), f32) * D_MODEL ** -0.5,
        'q_norm_a': 1.0 + 0.02 * nrm(ks[6], (DEPTH, HEAD_DIM), f32),
        'k_norm_a': 1.0 + 0.02 * nrm(ks[7], (DEPTH, HEAD_DIM), f32),
        'sink_a': 0.5 * nrm(ks[8], (DEPTH, A_HEADS), f32),
        'q_norm_b': 1.0 + 0.02 * nrm(ks[9], (DEPTH, HEAD_DIM), f32),
        'k_norm_b': 1.0 + 0.02 * nrm(ks[10], (DEPTH, HEAD_DIM), f32),
        'rpb_b': 0.1 * nrm(ks[11], (DEPTH, B_HEADS, 2 * NA_ROWS_MAX - 1, 2 * NA_COLS - 1), f32),
        'w_out': nrm(ks[12], (DEPTH, MIX_WIDTH, D_MODEL), f32) * MIX_WIDTH ** -0.5,
        'w_ple': nrm(ks[13], (DEPTH, PLE_DIM, D_MODEL), f32) * PLE_DIM ** -0.5,
        'w_ple_gate': nrm(ks[14], (DEPTH, D_MODEL, D_MODEL), f32) * D_MODEL ** -0.5,
    }


def reference(x_prompt, x_sample, p_prompt, p_sample, norm_w, w_in, q_norm_a, k_norm_a, sink_a,
              q_norm_b, k_norm_b, rpb_b, w_out, w_ple, w_ple_gate):
    y_prompt = run_trunk(x_prompt, p_prompt, norm_w, w_in, q_norm_a, k_norm_a, sink_a,
                         q_norm_b, k_norm_b, rpb_b, w_out, w_ple, w_ple_gate)
    y_sample = run_trunk(x_sample, p_sample, norm_w, w_in, q_norm_a, k_norm_a, sink_a,
                         q_norm_b, k_norm_b, rpb_b, w_out, w_ple, w_ple_gate)
    return (y_prompt, y_sample)
```

```python
import functools

import numpy as np
import jax
import jax.numpy as jnp
from jax import lax
from jax.experimental import pallas as pl
from jax.experimental.pallas import tpu as pltpu

D_MODEL = 1024
SEQ = 2048
HEAD_DIM = 64
A_HEADS = 8
A_KV_HEADS = 2
A_GROUP = A_HEADS // A_KV_HEADS
B_HEADS = 8
A_WIDTH = A_HEADS * HEAD_DIM
A_KV_WIDTH = A_KV_HEADS * HEAD_DIM
B_WIDTH = B_HEADS * HEAD_DIM
MIX_WIDTH = A_WIDTH + B_WIDTH
KV_WIDTH = A_KV_WIDTH + B_WIDTH
WINDOW = 128
ROT_DIM = HEAD_DIM // 4
ROT_HALF = ROT_DIM // 2
ROPE_THETA = 500000.0
GRID_W = 64
GRID_ROWS = SEQ // GRID_W
NA_ROWS = 8
NA_COLS = 16
PLE_DIM = 256
EPS = 1e-6
SCALE = HEAD_DIM ** -0.5

OFF_QA = 0
OFF_KA = OFF_QA + A_WIDTH
OFF_VA = OFF_KA + A_KV_WIDTH
OFF_GA = OFF_VA + A_KV_WIDTH
OFF_QB = OFF_GA + A_WIDTH
OFF_KB = OFF_QB + B_WIDTH
OFF_VB = OFF_KB + B_WIDTH
OFF_GB = OFF_VB + B_WIDTH
IN_WIDTH = OFF_GB + B_WIDTH

LANE = 128
N_BLK = SEQ // LANE
A_KEY_BLKS = 3
B_ROWS_PER_BLK = LANE // GRID_W
B_KEY_ROWS = 10
B_KEY_BLKS = B_KEY_ROWS * GRID_W // LANE
TM = 256
TQ = 256
VMEM_LIMIT_BYTES = 56 * 1024 * 1024

F32 = jnp.float32
BF16 = jnp.bfloat16
NEG_INF = float("-inf")


def _proj_kernel(x_ref, nw_ref, w_ref, qna_ref, kna_ref, qnb_ref, knb_ref, cos_ref, sin_ref,
                 qT_ref, vT_ref, sgT_ref, k_ref):
    x = x_ref[0]
    ms = jnp.mean(x * x, axis=-1, keepdims=True)
    h = (x * lax.rsqrt(ms + EPS) * nw_ref[...]).astype(BF16)

    def proj(lo, n):
        return lax.dot_general(w_ref[lo:lo + n, :], h, (((1,), (1,)), ((), ())),
                               preferred_element_type=F32)

    def head_norm(r, g_ref):
        r3 = r.reshape(r.shape[0] // HEAD_DIM, HEAD_DIM, r.shape[1])
        m2 = jnp.mean(r3 * r3, axis=1, keepdims=True)
        return r3 * lax.rsqrt(m2 + EPS) * g_ref[...][None]

    def rope(y):
        c = cos_ref[...][None]
        s = sin_ref[...][None]
        x1 = y[:, 0:ROT_HALF]
        x2 = y[:, ROT_HALF:ROT_DIM]
        return jnp.concatenate([x1 * c - x2 * s, x2 * c + x1 * s, y[:, ROT_DIM:]], axis=1)

    def silu(g):
        return g / (1.0 + jnp.exp(-g))

    tm = x.shape[0]
    qa = (rope(head_norm(proj(OFF_QA, A_WIDTH), qna_ref)) * SCALE).reshape(A_WIDTH, tm).astype(BF16)
    ka = rope(head_norm(proj(OFF_KA, A_KV_WIDTH), kna_ref)).reshape(A_KV_WIDTH, tm)
    va = proj(OFF_VA, A_KV_WIDTH).astype(BF16)
    sga = silu(proj(OFF_GA, A_WIDTH)).astype(BF16)
    qb = (head_norm(proj(OFF_QB, B_WIDTH), qnb_ref) * SCALE).reshape(B_WIDTH, tm).astype(BF16)
    kb = head_norm(proj(OFF_KB, B_WIDTH), knb_ref).reshape(B_WIDTH, tm)
    vb = proj(OFF_VB, B_WIDTH).astype(BF16)
    sgb = silu(proj(OFF_GB, B_WIDTH)).astype(BF16)

    k_ref[0, :, 0:A_KV_WIDTH] = ka.T.astype(BF16)
    k_ref[0, :, A_KV_WIDTH:KV_WIDTH] = kb.T.astype(BF16)
    for c in range(tm // LANE):
        sl = slice(c * LANE, (c + 1) * LANE)
        qT_ref[0, c, 0:A_WIDTH, :] = qa[:, sl]
        qT_ref[0, c, A_WIDTH:MIX_WIDTH, :] = qb[:, sl]
        vT_ref[0, c, 0:A_KV_WIDTH, :] = va[:, sl]
        vT_ref[0, c, A_KV_WIDTH:KV_WIDTH, :] = vb[:, sl]
        sgT_ref[0, c, 0:A_WIDTH, :] = sga[:, sl]
        sgT_ref[0, c, A_WIDTH:MIX_WIDTH, :] = sgb[:, sl]


def _project(x, nw, w_in_t, qna, kna, qnb, knb, cos_t, sin_t):
    b = x.shape[0]
    const = lambda *shape: pl.BlockSpec(shape, lambda i, j: (0,) * len(shape))
    blk = TM // LANE
    return pl.pallas_call(
        _proj_kernel,
        grid=(b, SEQ // TM),
        in_specs=[
            pl.BlockSpec((1, TM, D_MODEL), lambda i, j: (i, j, 0)),
            const(1, D_MODEL),
            const(IN_WIDTH, D_MODEL),
            const(HEAD_DIM, 1), const(HEAD_DIM, 1), const(HEAD_DIM, 1), const(HEAD_DIM, 1),
            pl.BlockSpec((ROT_HALF, TM), lambda i, j: (0, j)),
            pl.BlockSpec((ROT_HALF, TM), lambda i, j: (0, j)),
        ],
        out_specs=[
            pl.BlockSpec((1, blk, MIX_WIDTH, LANE), lambda i, j: (i, j, 0, 0)),
            pl.BlockSpec((1, blk, KV_WIDTH, LANE), lambda i, j: (i, j, 0, 0)),
            pl.BlockSpec((1, blk, MIX_WIDTH, LANE), lambda i, j: (i, j, 0, 0)),
            pl.BlockSpec((1, TM, KV_WIDTH), lambda i, j: (i, j, 0)),
        ],
        out_shape=[
            jax.ShapeDtypeStruct((b, N_BLK, MIX_WIDTH, LANE), BF16),
            jax.ShapeDtypeStruct((b, N_BLK, KV_WIDTH, LANE), BF16),
            jax.ShapeDtypeStruct((b, N_BLK, MIX_WIDTH, LANE), BF16),
            jax.ShapeDtypeStruct((b, SEQ, KV_WIDTH), BF16),
        ],
        compiler_params=pltpu.CompilerParams(
            dimension_semantics=("arbitrary", "arbitrary"), vmem_limit_bytes=VMEM_LIMIT_BYTES),
        name="proj",
    )(x, nw, w_in_t, qna, kna, qnb, knb, cos_t, sin_t)


def _attn_kernel(idx_ref, qT_ref, sgT_ref, k_ref, vT_ref, x_ref, p_ref, wout_ref, wgate_ref, wple_ref,
                 mask_ref, sink_ref, bias_ref, y_ref, oT_ref):
    t = pl.program_id(1)
    nblk = TQ // LANE
    zq = jnp.zeros((HEAD_DIM, LANE), BF16)

    for jj in range(nblk):
        jb = t * nblk + jj
        lanes = slice(jj * LANE, (jj + 1) * LANE)

        kb0 = jnp.clip(jb - 1, 0, N_BLK - A_KEY_BLKS)
        variant = jnp.where(jb == 0, 0, jnp.where(jb == N_BLK - 1, 2, 1))
        k_a = k_ref[0, pl.ds(pl.multiple_of(kb0 * LANE, LANE), A_KEY_BLKS * LANE), 0:A_KV_WIDTH]
        v_a = jnp.concatenate([vT_ref[0, kb0 + i, 0:A_KV_WIDTH, :] for i in range(A_KEY_BLKS)], axis=1)
        mask = mask_ref[variant]
        mask4 = jnp.concatenate([mask] * A_GROUP, axis=1)
        for g in range(A_KV_HEADS):
            heads = [A_GROUP * g + i for i in range(A_GROUP)]
            qcat = jnp.concatenate(
                [qT_ref[0, jj, h * HEAD_DIM:(h + 1) * HEAD_DIM, :] for h in heads], axis=1)
            zc = jnp.zeros_like(qcat)
            rhs = jnp.concatenate([qcat, zc] if g == 0 else [zc, qcat], axis=0)
            s = jnp.dot(k_a, rhs, preferred_element_type=F32) + mask4
            sink = sink_ref[g]
            m = jnp.maximum(jnp.max(s, axis=0, keepdims=True), sink)
            e = jnp.exp(s - m)
            den = jnp.sum(e, axis=0, keepdims=True) + jnp.exp(sink - m)
            o = jnp.dot(v_a[g * HEAD_DIM:(g + 1) * HEAD_DIM, :], e.astype(BF16),
                        preferred_element_type=F32) / den
            for i, h in enumerate(heads):
                oT_ref[h * HEAD_DIM:(h + 1) * HEAD_DIM, lanes] = o[:, i * LANE:(i + 1) * LANE]

        kbb = jnp.clip(jb - 2, 0, N_BLK - B_KEY_BLKS)
        kstart = pl.multiple_of(kbb * LANE, LANE)
        for hp in range(B_HEADS // 2):
            lo = A_KV_WIDTH + hp * LANE
            k_b = k_ref[0, pl.ds(kstart, B_KEY_BLKS * LANE), lo:lo + LANE]
            qlo = A_WIDTH + hp * LANE
            q0 = qT_ref[0, jj, qlo:qlo + HEAD_DIM, :]
            q1 = qT_ref[0, jj, qlo + HEAD_DIM:qlo + LANE, :]
            rhs = jnp.concatenate([jnp.concatenate([q0, zq], axis=1),
                                   jnp.concatenate([zq, q1], axis=1)], axis=0)
            bias = jnp.concatenate(
                [jnp.concatenate([bias_ref[idx_ref[jb * B_KEY_ROWS + a], 2 * hp],
                                  bias_ref[idx_ref[jb * B_KEY_ROWS + a], 2 * hp + 1]], axis=1)
                 for a in range(B_KEY_ROWS)], axis=0)
            s = jnp.dot(k_b, rhs, preferred_element_type=F32) + bias
            m = jnp.max(s, axis=0, keepdims=True)
            e = jnp.exp(s - m)
            den = jnp.sum(e, axis=0, keepdims=True)
            v_b = jnp.concatenate([vT_ref[0, kbb + i, lo:lo + LANE, :] for i in range(B_KEY_BLKS)], axis=1)
            o = jnp.dot(v_b, e.astype(BF16), preferred_element_type=F32) / den
            h0 = A_WIDTH + hp * LANE
            oT_ref[h0:h0 + HEAD_DIM, lanes] = o[0:HEAD_DIM, 0:LANE]
            oT_ref[h0 + HEAD_DIM:h0 + LANE, lanes] = o[HEAD_DIM:LANE, LANE:2 * LANE]

    sg = jnp.concatenate([sgT_ref[0, jj] for jj in range(nblk)], axis=1).astype(F32)
    og = (oT_ref[...] * sg).T.astype(BF16)
    y1 = x_ref[0] + jnp.dot(og, wout_ref[...], preferred_element_type=F32)
    z = jnp.dot(y1.astype(BF16), wgate_ref[...], preferred_element_type=F32)
    gate = 1.0 / (1.0 + jnp.exp(-z))
    ple = jnp.dot(p_ref[0].astype(BF16), wple_ref[...], preferred_element_type=F32)
    y_ref[0] = y1 + ple * gate


def _attend(idx, qT, sgT, k, vT, x, p, w_out, w_gate, w_ple, mask_a, sink_rows, bias_tab):
    b = x.shape[0]
    const = lambda *shape: pl.BlockSpec(shape, lambda i, j, idx_ref: (0,) * len(shape))
    blk = TQ // LANE
    grid_spec = pltpu.PrefetchScalarGridSpec(
        num_scalar_prefetch=1,
        grid=(b, SEQ // TQ),
        in_specs=[
            pl.BlockSpec((1, blk, MIX_WIDTH, LANE), lambda i, j, idx_ref: (i, j, 0, 0)),
            pl.BlockSpec((1, blk, MIX_WIDTH, LANE), lambda i, j, idx_ref: (i, j, 0, 0)),
            pl.BlockSpec((1, SEQ, KV_WIDTH), lambda i, j, idx_ref: (i, 0, 0)),
            pl.BlockSpec((1, N_BLK, KV_WIDTH, LANE), lambda i, j, idx_ref: (i, 0, 0, 0)),
            pl.BlockSpec((1, TQ, D_MODEL), lambda i, j, idx_ref: (i, j, 0)),
            pl.BlockSpec((1, TQ, PLE_DIM), lambda i, j, idx_ref: (i, j, 0)),
            const(MIX_WIDTH, D_MODEL),
            const(D_MODEL, D_MODEL),
            const(PLE_DIM, D_MODEL),
            const(*mask_a.shape),
            const(*sink_rows.shape),
            const(*bias_tab.shape),
        ],
        out_specs=pl.BlockSpec((1, TQ, D_MODEL), lambda i, j, idx_ref: (i, j, 0)),
        scratch_shapes=[pltpu.VMEM((MIX_WIDTH, TQ), F32)],
    )
    return pl.pallas_call(
        _attn_kernel,
        grid_spec=grid_spec,
        out_shape=jax.ShapeDtypeStruct((b, SEQ, D_MODEL), F32),
        compiler_params=pltpu.CompilerParams(
            dimension_semantics=("arbitrary", "arbitrary"), vmem_limit_bytes=VMEM_LIMIT_BYTES),
        name="attn",
    )(idx, qT, sgT, k, vT, x, p, w_out, w_gate, w_ple, mask_a, sink_rows, bias_tab)


def _rope_tables():
    inv_freq = jnp.power(ROPE_THETA, -jnp.arange(0, ROT_DIM, 2, dtype=F32) / ROT_DIM)
    ang = jnp.arange(SEQ).astype(F32)[:, None] * inv_freq[None, :]
    return jnp.cos(ang).T, jnp.sin(ang).T


def _mask_a():
    i = np.arange(A_KEY_BLKS * LANE)[:, None]
    q = np.arange(LANE)[None, :]
    out = []
    for jb in (0, 1, N_BLK - 1):
        kb0 = min(max(jb - 1, 0), N_BLK - A_KEY_BLKS)
        diff = (kb0 * LANE + i) - (jb * LANE + q)
        out.append(np.where(np.abs(diff) <= WINDOW, 0.0, NEG_INF))
    return np.stack(out).astype(np.float32)


def _bias_plan():
    entries = {}
    idx = np.zeros((GRID_ROWS // B_ROWS_PER_BLK, B_KEY_ROWS), np.int32)
    for rp in range(GRID_ROWS // B_ROWS_PER_BLK):
        r = B_ROWS_PER_BLK * rp
        kr0 = min(max(r - NA_ROWS // 2, 0), GRID_ROWS - B_KEY_ROWS)
        rs_l = min(max(r - NA_ROWS // 2, 0), GRID_ROWS - NA_ROWS)
        rs_r = min(max(r + 1 - NA_ROWS // 2, 0), GRID_ROWS - NA_ROWS)
        for a in range(B_KEY_ROWS):
            kr = kr0 + a
            v_l = rs_l <= kr < rs_l + NA_ROWS
            v_r = rs_r <= kr < rs_r + NA_ROWS
            d_l = kr - r + NA_ROWS - 1
            key = (bool(v_l), bool(v_r), d_l if (v_l or v_r) else 0)
            idx[rp, a] = entries.setdefault(key, len(entries))
    return list(entries.keys()), idx.reshape(-1)


def _bias_table(rpb, entries):
    ck = np.arange(GRID_W)[:, None]
    cq = np.arange(GRID_W)[None, :]
    cs = np.clip(cq - NA_COLS // 2, 0, GRID_W - NA_COLS)
    valid = (ck >= cs) & (ck < cs + NA_COLS)
    dcol = np.clip(ck - cq, -(NA_COLS - 1), NA_COLS - 1) + NA_COLS - 1
    tab = jnp.where(valid[None, None], rpb[:, :, dcol], NEG_INF)
    neg = jnp.full((B_HEADS, GRID_W, GRID_W), NEG_INF, F32)
    blocks = []
    for v_l, v_r, d_l in entries:
        left = tab[:, d_l] if v_l else neg
        right = tab[:, d_l - 1] if v_r else neg
        blocks.append(jnp.concatenate([left, right], axis=-1))
    return jnp.stack(blocks)


def _layer(x, p, consts):
    (nw, w_in_t, qna, kna, qnb, knb, cos_t, sin_t, idx, w_out, w_gate, w_ple, mask_a, sink_rows,
     bias_tab) = consts
    qT, vT, sgT, k = _project(x, nw, w_in_t, qna, kna, qnb, knb, cos_t, sin_t)
    return _attend(idx, qT, sgT, k, vT, x, p, w_out, w_gate, w_ple, mask_a, sink_rows, bias_tab)


def kernel(x_prompt, x_sample, p_prompt, p_sample, norm_w, w_in, q_norm_a, k_norm_a, sink_a,
           q_norm_b, k_norm_b, rpb_b, w_out, w_ple, w_ple_gate):
    assert x_prompt.shape[1:] == (SEQ, D_MODEL) and x_sample.shape[1:] == (SEQ, D_MODEL)
    assert norm_w.shape[0] == 1, "one layer"
    col = lambda v: v[0].reshape(HEAD_DIM, 1).astype(F32)
    cos_t, sin_t = _rope_tables()
    entries, idx = _bias_plan()
    consts = (
        norm_w[0].reshape(1, D_MODEL),
        w_in[0].T.astype(BF16),
        col(q_norm_a), col(k_norm_a), col(q_norm_b), col(k_norm_b),
        cos_t, sin_t,
        jnp.asarray(idx),
        w_out[0].astype(BF16), w_ple_gate[0].astype(BF16), w_ple[0].astype(BF16),
        jnp.asarray(_mask_a()),
        jnp.repeat(sink_a[0].astype(F32), LANE).reshape(A_KV_HEADS, 1, A_GROUP * LANE),
        _bias_table(rpb_b[0].astype(F32), entries),
    )
    return (_layer(x_prompt, p_prompt[0], consts), _layer(x_sample, p_sample[0], consts))
```

```python
import functools

import numpy as np
import jax
import jax.numpy as jnp
from jax import lax
from jax.experimental import pallas as pl
from jax.experimental.pallas import tpu as pltpu

D_MODEL = 1024
SEQ = 2048
HEAD_DIM = 64
A_HEADS = 8
A_KV_HEADS = 2
A_GROUP = A_HEADS // A_KV_HEADS
B_HEADS = 8
A_WIDTH = A_HEADS * HEAD_DIM
A_KV_WIDTH = A_KV_HEADS * HEAD_DIM
B_WIDTH = B_HEADS * HEAD_DIM
MIX_WIDTH = A_WIDTH + B_WIDTH
KV_WIDTH = A_KV_WIDTH + B_WIDTH
WINDOW = 128
ROT_DIM = HEAD_DIM // 4
ROT_HALF = ROT_DIM // 2
ROPE_THETA = 500000.0
GRID_W = 64
GRID_ROWS = SEQ // GRID_W
NA_ROWS = 8
NA_COLS = 16
PLE_DIM = 256
EPS = 1e-6
SCALE = HEAD_DIM ** -0.5
LOG2E = float(np.log2(np.e))

OFF_QA = 0
OFF_KA = OFF_QA + A_WIDTH
OFF_VA = OFF_KA + A_KV_WIDTH
OFF_GA = OFF_VA + A_KV_WIDTH
OFF_QB = OFF_GA + A_WIDTH
OFF_KB = OFF_QB + B_WIDTH
OFF_VB = OFF_KB + B_WIDTH
OFF_GB = OFF_VB + B_WIDTH
IN_WIDTH = OFF_GB + B_WIDTH

LANE = 128
BF16_SUBLANES = 16
N_BLK = SEQ // LANE
A_KEY_BLKS = 3
B_ROWS_PER_BLK = LANE // GRID_W
B_KEY_ROWS = 10
B_KEY_BLKS = B_KEY_ROWS * GRID_W // LANE
TM = 256
TQ = 256
VMEM_LIMIT_BYTES = 56 * 1024 * 1024

F32 = jnp.float32
BF16 = jnp.bfloat16
NEG_INF = float("-inf")


def _proj_kernel(x_ref, nw_ref, w_ref, qna_ref, kna_ref, qnb_ref, knb_ref, cos_ref, sin_ref,
                 qT_ref, vT_ref, sgT_ref, k_ref):
    x = x_ref[0]
    ms = jnp.mean(x * x, axis=-1, keepdims=True)
    h = (x * lax.rsqrt(ms + EPS) * nw_ref[...]).astype(BF16)

    def proj(lo, n):
        return lax.dot_general(w_ref[lo:lo + n, :], h, (((1,), (1,)), ((), ())),
                               preferred_element_type=F32)

    def head_norm(r, g_ref):
        r3 = r.reshape(r.shape[0] // HEAD_DIM, HEAD_DIM, r.shape[1])
        m2 = jnp.mean(r3 * r3, axis=1, keepdims=True)
        return r3 * lax.rsqrt(m2 + EPS) * g_ref[...][None]

    def rope(y):
        c = cos_ref[...][None]
        s = sin_ref[...][None]
        x1 = y[:, 0:ROT_HALF]
        x2 = y[:, ROT_HALF:ROT_DIM]
        return jnp.concatenate([x1 * c - x2 * s, x2 * c + x1 * s, y[:, ROT_DIM:]], axis=1)

    def silu(g):
        return g / (1.0 + jnp.exp(-g))

    tm = x.shape[0]
    qa = (rope(head_norm(proj(OFF_QA, A_WIDTH), qna_ref)) * (SCALE * LOG2E)).reshape(A_WIDTH, tm).astype(BF16)
    ka = rope(head_norm(proj(OFF_KA, A_KV_WIDTH), kna_ref)).reshape(A_KV_WIDTH, tm)
    va = proj(OFF_VA, A_KV_WIDTH).astype(BF16)
    sga = silu(proj(OFF_GA, A_WIDTH)).astype(BF16)
    qb = (head_norm(proj(OFF_QB, B_WIDTH), qnb_ref) * (SCALE * LOG2E)).reshape(B_WIDTH, tm).astype(BF16)
    kb = head_norm(proj(OFF_KB, B_WIDTH), knb_ref).reshape(B_WIDTH, tm)
    vb = proj(OFF_VB, B_WIDTH).astype(BF16)
    sgb = silu(proj(OFF_GB, B_WIDTH)).astype(BF16)

    k_ref[0, :, 0:A_KV_WIDTH] = ka.T.astype(BF16)
    k_ref[0, :, A_KV_WIDTH:KV_WIDTH] = kb.T.astype(BF16)
    for c in range(tm // LANE):
        sl = slice(c * LANE, (c + 1) * LANE)
        qT_ref[0, c, 0:A_WIDTH, :] = qa[:, sl]
        qT_ref[0, c, A_WIDTH:MIX_WIDTH, :] = qb[:, sl]
        vT_ref[0, c, 0:A_KV_WIDTH, :] = va[:, sl]
        vT_ref[0, c, A_KV_WIDTH:KV_WIDTH, :] = vb[:, sl]
        sgT_ref[0, c, 0:A_WIDTH, :] = sga[:, sl]
        sgT_ref[0, c, A_WIDTH:MIX_WIDTH, :] = sgb[:, sl]


def _project(x, nw, w_in_t, qna, kna, qnb, knb, cos_t, sin_t):
    b = x.shape[0]
    const = lambda *shape: pl.BlockSpec(shape, lambda i, j: (0,) * len(shape))
    blk = TM // LANE
    return pl.pallas_call(
        _proj_kernel,
        grid=(b, SEQ // TM),
        in_specs=[
            pl.BlockSpec((1, TM, D_MODEL), lambda i, j: (i, j, 0)),
            const(1, D_MODEL),
            const(IN_WIDTH, D_MODEL),
            const(HEAD_DIM, 1), const(HEAD_DIM, 1), const(HEAD_DIM, 1), const(HEAD_DIM, 1),
            pl.BlockSpec((ROT_HALF, TM), lambda i, j: (0, j)),
            pl.BlockSpec((ROT_HALF, TM), lambda i, j: (0, j)),
        ],
        out_specs=[
            pl.BlockSpec((1, blk, MIX_WIDTH, LANE), lambda i, j: (i, j, 0, 0)),
            pl.BlockSpec((1, blk, KV_WIDTH, LANE), lambda i, j: (i, j, 0, 0)),
            pl.BlockSpec((1, blk, MIX_WIDTH, LANE), lambda i, j: (i, j, 0, 0)),
            pl.BlockSpec((1, TM, KV_WIDTH), lambda i, j: (i, j, 0)),
        ],
        out_shape=[
            jax.ShapeDtypeStruct((b, N_BLK, MIX_WIDTH, LANE), BF16),
            jax.ShapeDtypeStruct((b, N_BLK, KV_WIDTH, LANE), BF16),
            jax.ShapeDtypeStruct((b, N_BLK, MIX_WIDTH, LANE), BF16),
            jax.ShapeDtypeStruct((b, SEQ, KV_WIDTH), BF16),
        ],
        compiler_params=pltpu.CompilerParams(
            dimension_semantics=("arbitrary", "arbitrary"), vmem_limit_bytes=VMEM_LIMIT_BYTES),
        name="proj",
    )(x, nw, w_in_t, qna, kna, qnb, knb, cos_t, sin_t)


def _attn_kernel(idx_ref, qT_ref, sgT_ref, k_ref, vT_ref, x_ref, p_ref, wout_ref, wgate_ref, wple_ref,
                 mask_ref, sink_ref, bias_ref, y_ref, oT_ref, s_ref, pr_ref):
    t_idx = pl.program_id(1)
    nblk = TQ // LANE
    zq = jnp.zeros((HEAD_DIM, LANE), BF16)
    ones_rows = jnp.ones((BF16_SUBLANES, LANE), BF16)
    sub = LANE // 8

    units = []
    for jj in range(nblk):
        a_units = [("A", jj, g, half) for g in range(A_KV_HEADS) for half in range(2)]
        b_units = [("B", jj, hp, 0) for hp in range(B_HEADS // 2)]
        for a_u, b_u in zip(a_units, b_units):
            units += [a_u, b_u]
    state = [None] * len(units)

    def prep(u):
        kind, jj, i0, i1 = units[u]
        jb = t_idx * nblk + jj
        st = {"kind": kind, "jj": jj, "buf": u % 2}
        if kind == "A":
            g, half = i0, i1
            h0 = A_GROUP * g + 2 * half
            qcat = jnp.concatenate([qT_ref[0, jj, h0 * HEAD_DIM:(h0 + 1) * HEAD_DIM, :],
                                    qT_ref[0, jj, (h0 + 1) * HEAD_DIM:(h0 + 2) * HEAD_DIM, :]], axis=1)
            zc = jnp.zeros_like(qcat)
            st["rhs"] = jnp.concatenate([qcat, zc] if g == 0 else [zc, qcat], axis=0)
            st["n"] = A_KEY_BLKS
            st["kb0"] = jnp.clip(jb - 1, 0, N_BLK - A_KEY_BLKS)
            st["klanes"] = slice(0, A_KV_WIDTH)
            st["vrows"] = slice(g * HEAD_DIM, (g + 1) * HEAD_DIM)
            st["variant"] = jnp.where(jb == 0, 0, jnp.where(jb == N_BLK - 1, 2, 1))
            st["sink"] = sink_ref[2 * g + half]
            st["m"] = jnp.broadcast_to(st["sink"], (8, 2 * LANE))
            st["out"] = [(h0 * HEAD_DIM, slice(0, HEAD_DIM), slice(0, LANE)),
                         ((h0 + 1) * HEAD_DIM, slice(0, HEAD_DIM), slice(LANE, 2 * LANE))]
        else:
            hp = i0
            lo = A_KV_WIDTH + hp * LANE
            qlo = A_WIDTH + hp * LANE
            q0 = qT_ref[0, jj, qlo:qlo + HEAD_DIM, :]
            q1 = qT_ref[0, jj, qlo + HEAD_DIM:qlo + LANE, :]
            st["rhs"] = jnp.concatenate([jnp.concatenate([q0, zq], axis=1),
                                         jnp.concatenate([zq, q1], axis=1)], axis=0)
            st["n"] = B_KEY_BLKS
            st["kb0"] = jnp.clip(jb - 2, 0, N_BLK - B_KEY_BLKS)
            st["klanes"] = slice(lo, lo + LANE)
            st["vrows"] = slice(lo, lo + LANE)
            st["hp"] = hp
            st["jb"] = jb
            st["m"] = jnp.full((8, 2 * LANE), NEG_INF, F32)
            st["out"] = [(qlo, slice(0, HEAD_DIM), slice(0, LANE)),
                         (qlo + HEAD_DIM, slice(HEAD_DIM, LANE), slice(LANE, 2 * LANE))]
        state[u] = st

    def s_chunk(u, c):
        st = state[u]
        rows = pl.ds(pl.multiple_of((st["kb0"] + c) * LANE, LANE), LANE)
        t = jnp.dot(k_ref[0, rows, st["klanes"]], st["rhs"], preferred_element_type=F32)
        if st["kind"] == "A":
            mk = mask_ref[st["variant"], c]
            bias = jnp.concatenate([mk, mk], axis=1)
        else:
            e0 = idx_ref[st["jb"] * B_KEY_ROWS + 2 * c]
            e1 = idx_ref[st["jb"] * B_KEY_ROWS + 2 * c + 1]
            hp = st["hp"]
            bias = jnp.concatenate(
                [jnp.concatenate([bias_ref[e0, 2 * hp], bias_ref[e0, 2 * hp + 1]], axis=1),
                 jnp.concatenate([bias_ref[e1, 2 * hp], bias_ref[e1, 2 * hp + 1]], axis=1)], axis=0)
        t = t + bias
        s_ref[st["buf"], c * LANE:(c + 1) * LANE, :] = t
        st["m"] = jnp.maximum(st["m"], jnp.max(t.reshape(sub, 8, 2 * LANE), axis=0))

    def s_finish(u):
        st = state[u]
        st["m"] = jnp.broadcast_to(jnp.max(st["m"], axis=0, keepdims=True), (8, 2 * LANE))

    def e_chunk(u, c):
        st = state[u]
        t = s_ref[st["buf"], c * LANE:(c + 1) * LANE, :]
        p = jnp.exp2(t.reshape(sub, 8, 2 * LANE) - st["m"][None]).reshape(LANE, 2 * LANE)
        pr_ref[st["buf"], c * LANE:(c + 1) * LANE, :] = p.astype(BF16)

    def pv(u):
        st = state[u]
        n = st["n"]
        v = jnp.concatenate(
            [jnp.concatenate([vT_ref[0, st["kb0"] + i, st["vrows"], :], ones_rows], axis=0)
             for i in range(n)], axis=1)
        o = jnp.dot(v, pr_ref[st["buf"], 0:n * LANE, :], preferred_element_type=F32)
        nv = v.shape[0] - BF16_SUBLANES
        den = o[nv:nv + 1, :]
        if st["kind"] == "A":
            den = den + jnp.exp2(st["sink"] - st["m"][0:1, :])
        r = 1.0 / den
        lanes = slice(st["jj"] * LANE, (st["jj"] + 1) * LANE)
        for row0, rsl, lsl in st["out"]:
            oT_ref[row0:row0 + HEAD_DIM, lanes] = o[rsl, lsl] * r[:, lsl]
        state[u] = None

    n_units = len(units)
    prep(0)
    for c in range(state[0]["n"]):
        s_chunk(0, c)
    s_finish(0)
    for u in range(n_units):
        nxt = u + 1 if u + 1 < n_units else None
        if nxt is not None:
            prep(nxt)
        n_s = state[nxt]["n"] if nxt is not None else 0
        n_e = state[u]["n"]
        for c in range(max(n_s, n_e)):
            if c < n_s:
                s_chunk(nxt, c)
            if c < n_e:
                e_chunk(u, c)
        if nxt is not None:
            s_finish(nxt)
        pv(u)

    sg = jnp.concatenate([sgT_ref[0, jj] for jj in range(nblk)], axis=1).astype(F32)
    og = (oT_ref[...] * sg).T.astype(BF16)
    y1 = x_ref[0] + jnp.dot(og, wout_ref[...], preferred_element_type=F32)
    z = jnp.dot(y1.astype(BF16), wgate_ref[...], preferred_element_type=F32)
    gate = 1.0 / (1.0 + jnp.exp(-z))
    ple = jnp.dot(p_ref[0].astype(BF16), wple_ref[...], preferred_element_type=F32)
    y_ref[0] = y1 + ple * gate


def _attend(idx, qT, sgT, k, vT, x, p, w_out, w_gate, w_ple, mask_a, sink_rows, bias_tab):
    b = x.shape[0]
    const = lambda *shape: pl.BlockSpec(shape, lambda i, j, idx_ref: (0,) * len(shape))
    blk = TQ // LANE
    grid_spec = pltpu.PrefetchScalarGridSpec(
        num_scalar_prefetch=1,
        grid=(b, SEQ // TQ),
        in_specs=[
            pl.BlockSpec((1, blk, MIX_WIDTH, LANE), lambda i, j, idx_ref: (i, j, 0, 0)),
            pl.BlockSpec((1, blk, MIX_WIDTH, LANE), lambda i, j, idx_ref: (i, j, 0, 0)),
            pl.BlockSpec((1, SEQ, KV_WIDTH), lambda i, j, idx_ref: (i, 0, 0)),
            pl.BlockSpec((1, N_BLK, KV_WIDTH, LANE), lambda i, j, idx_ref: (i, 0, 0, 0)),
            pl.BlockSpec((1, TQ, D_MODEL), lambda i, j, idx_ref: (i, j, 0)),
            pl.BlockSpec((1, TQ, PLE_DIM), lambda i, j, idx_ref: (i, j, 0)),
            const(MIX_WIDTH, D_MODEL),
            const(D_MODEL, D_MODEL),
            const(PLE_DIM, D_MODEL),
            const(*mask_a.shape),
            const(*sink_rows.shape),
            const(*bias_tab.shape),
        ],
        out_specs=pl.BlockSpec((1, TQ, D_MODEL), lambda i, j, idx_ref: (i, j, 0)),
        scratch_shapes=[
            pltpu.VMEM((MIX_WIDTH, TQ), F32),
            pltpu.VMEM((2, B_KEY_BLKS * LANE, 2 * LANE), F32),
            pltpu.VMEM((2, B_KEY_BLKS * LANE, 2 * LANE), BF16),
        ],
    )
    return pl.pallas_call(
        _attn_kernel,
        grid_spec=grid_spec,
        out_shape=jax.ShapeDtypeStruct((b, SEQ, D_MODEL), F32),
        compiler_params=pltpu.CompilerParams(
            dimension_semantics=("arbitrary", "arbitrary"), vmem_limit_bytes=VMEM_LIMIT_BYTES),
        name="attn",
    )(idx, qT, sgT, k, vT, x, p, w_out, w_gate, w_ple, mask_a, sink_rows, bias_tab)


def _rope_tables():
    inv_freq = jnp.power(ROPE_THETA, -jnp.arange(0, ROT_DIM, 2, dtype=F32) / ROT_DIM)
    ang = jnp.arange(SEQ).astype(F32)[:, None] * inv_freq[None, :]
    return jnp.cos(ang).T, jnp.sin(ang).T


def _mask_a():
    i = np.arange(A_KEY_BLKS * LANE)[:, None]
    q = np.arange(LANE)[None, :]
    out = []
    for jb in (0, 1, N_BLK - 1):
        kb0 = min(max(jb - 1, 0), N_BLK - A_KEY_BLKS)
        diff = (kb0 * LANE + i) - (jb * LANE + q)
        out.append(np.where(np.abs(diff) <= WINDOW, 0.0, NEG_INF).reshape(A_KEY_BLKS, LANE, LANE))
    return np.stack(out).astype(np.float32)


def _bias_plan():
    entries = {}
    idx = np.zeros((GRID_ROWS // B_ROWS_PER_BLK, B_KEY_ROWS), np.int32)
    for rp in range(GRID_ROWS // B_ROWS_PER_BLK):
        r = B_ROWS_PER_BLK * rp
        kr0 = min(max(r - NA_ROWS // 2, 0), GRID_ROWS - B_KEY_ROWS)
        rs_l = min(max(r - NA_ROWS // 2, 0), GRID_ROWS - NA_ROWS)
        rs_r = min(max(r + 1 - NA_ROWS // 2, 0), GRID_ROWS - NA_ROWS)
        for a in range(B_KEY_ROWS):
            kr = kr0 + a
            v_l = rs_l <= kr < rs_l + NA_ROWS
            v_r = rs_r <= kr < rs_r + NA_ROWS
            d_l = kr - r + NA_ROWS - 1
            key = (bool(v_l), bool(v_r), d_l if (v_l or v_r) else 0)
            idx[rp, a] = entries.setdefault(key, len(entries))
    return list(entries.keys()), idx.reshape(-1)


def _bias_table(rpb, entries):
    ck = np.arange(GRID_W)[:, None]
    cq = np.arange(GRID_W)[None, :]
    cs = np.clip(cq - NA_COLS // 2, 0, GRID_W - NA_COLS)
    valid = (ck >= cs) & (ck < cs + NA_COLS)
    dcol = np.clip(ck - cq, -(NA_COLS - 1), NA_COLS - 1) + NA_COLS - 1
    tab = jnp.where(valid[None, None], rpb[:, :, dcol], NEG_INF)
    neg = jnp.full((B_HEADS, GRID_W, GRID_W), NEG_INF, F32)
    blocks = []
    for v_l, v_r, d_l in entries:
        left = tab[:, d_l] if v_l else neg
        right = tab[:, d_l - 1] if v_r else neg
        blocks.append(jnp.concatenate([left, right], axis=-1))
    return jnp.stack(blocks)


def _layer(x, p, consts):
    (nw, w_in_t, qna, kna, qnb, knb, cos_t, sin_t, idx, w_out, w_gate, w_ple, mask_a, sink_rows,
     bias_tab) = consts
    qT, vT, sgT, k = _project(x, nw, w_in_t, qna, kna, qnb, knb, cos_t, sin_t)
    return _attend(idx, qT, sgT, k, vT, x, p, w_out, w_gate, w_ple, mask_a, sink_rows, bias_tab)


def kernel(x_prompt, x_sample, p_prompt, p_sample, norm_w, w_in, q_norm_a, k_norm_a, sink_a,
           q_norm_b, k_norm_b, rpb_b, w_out, w_ple, w_ple_gate):
    assert x_prompt.shape[1:] == (SEQ, D_MODEL) and x_sample.shape[1:] == (SEQ, D_MODEL)
    assert norm_w.shape[0] == 1, "one layer"
    col = lambda v: v[0].reshape(HEAD_DIM, 1).astype(F32)
    cos_t, sin_t = _rope_tables()
    entries, idx = _bias_plan()
    consts = (
        norm_w[0].reshape(1, D_MODEL),
        w_in[0].T.astype(BF16),
        col(q_norm_a), col(k_norm_a), col(q_norm_b), col(k_norm_b),
        cos_t, sin_t,
        jnp.asarray(idx),
        w_out[0].astype(BF16), w_ple_gate[0].astype(BF16), w_ple[0].astype(BF16),
        jnp.asarray(_mask_a()),
        jnp.repeat(sink_a[0].astype(F32) * LOG2E, LANE).reshape(A_HEADS // 2, 1, 2 * LANE),
        _bias_table(rpb_b[0].astype(F32) * LOG2E, entries),
    )
    return (_layer(x_prompt, p_prompt[0], consts), _layer(x_sample, p_sample[0], consts))
```

```python
import functools

import numpy as np
import jax
import jax.numpy as jnp
from jax import lax
from jax.experimental import pallas as pl
from jax.experimental.pallas import tpu as pltpu

D_MODEL = 1024
SEQ = 2048
HEAD_DIM = 64
A_HEADS = 8
A_KV_HEADS = 2
A_GROUP = A_HEADS // A_KV_HEADS
B_HEADS = 8
A_WIDTH = A_HEADS * HEAD_DIM
A_KV_WIDTH = A_KV_HEADS * HEAD_DIM
B_WIDTH = B_HEADS * HEAD_DIM
MIX_WIDTH = A_WIDTH + B_WIDTH
KV_WIDTH = A_KV_WIDTH + B_WIDTH
WINDOW = 128
ROT_DIM = HEAD_DIM // 4
ROT_HALF = ROT_DIM // 2
ROPE_THETA = 500000.0
GRID_W = 64
GRID_ROWS = SEQ // GRID_W
NA_ROWS = 8
NA_COLS = 16
PLE_DIM = 256
EPS = 1e-6
SCALE = HEAD_DIM ** -0.5
LOG2E = float(np.log2(np.e))

OFF_QA = 0
OFF_KA = OFF_QA + A_WIDTH
OFF_VA = OFF_KA + A_KV_WIDTH
OFF_GA = OFF_VA + A_KV_WIDTH
OFF_QB = OFF_GA + A_WIDTH
OFF_KB = OFF_QB + B_WIDTH
OFF_VB = OFF_KB + B_WIDTH
OFF_GB = OFF_VB + B_WIDTH
IN_WIDTH = OFF_GB + B_WIDTH

LANE = 128
BF16_SUBLANES = 16
N_BLK = SEQ // LANE
A_KEY_BLKS = 3
B_ROWS_PER_BLK = LANE // GRID_W
B_KEY_ROWS = 10
B_KEY_BLKS = B_KEY_ROWS * GRID_W // LANE
TM = 512
TSUB = 256
TQ = 512
PROJ_N = 256
PV_CHUNKS = 2
S_AHEAD = 2
VMEM_LIMIT_BYTES = 56 * 1024 * 1024

F32 = jnp.float32
BF16 = jnp.bfloat16
NEG_INF = float("-inf")


def _proj_kernel(x_ref, nw_ref, w_ref, qna_ref, kna_ref, qnb_ref, knb_ref, cos_ref, sin_ref,
                 qT_ref, vT_ref, sgT_ref, k_ref):
    n_sub = x_ref.shape[1] // TSUB

    def norm_h(i):
        x = x_ref[0, i * TSUB:(i + 1) * TSUB, :]
        ms = jnp.mean(x * x, axis=-1, keepdims=True)
        return (x * lax.rsqrt(ms + EPS) * nw_ref[...]).astype(BF16)

    def proj(h, lo, n):
        return lax.dot_general(w_ref[lo:lo + n, :], h, (((1,), (1,)), ((), ())),
                               preferred_element_type=F32)

    def head_norm(r, g_ref):
        r3 = r.reshape(r.shape[0] // HEAD_DIM, HEAD_DIM, r.shape[1])
        m2 = jnp.mean(r3 * r3, axis=1, keepdims=True)
        return r3 * lax.rsqrt(m2 + EPS) * g_ref[...][None]

    def rope(y, i):
        c = cos_ref[:, i * TSUB:(i + 1) * TSUB][None]
        s = sin_ref[:, i * TSUB:(i + 1) * TSUB][None]
        x1 = y[:, 0:ROT_HALF]
        x2 = y[:, ROT_HALF:ROT_DIM]
        return jnp.concatenate([x1 * c - x2 * s, x2 * c + x1 * s, y[:, ROT_DIM:]], axis=1)

    def silu(g):
        return g / (1.0 + jnp.exp(-g))

    def put(ref, i, row0, val):
        for c in range(TSUB // LANE):
            ref[0, i * (TSUB // LANE) + c, row0:row0 + val.shape[0], :] = val[:, c * LANE:(c + 1) * LANE]

    def seg_ga(i, h):
        put(sgT_ref, i, 0, silu(proj(h, OFF_GA, A_WIDTH)).astype(BF16))

    def seg_gb(i, h):
        put(sgT_ref, i, A_WIDTH, silu(proj(h, OFF_GB, B_WIDTH)).astype(BF16))

    def seg_qa(i, h):
        q = rope(head_norm(proj(h, OFF_QA, A_WIDTH), qna_ref), i) * (SCALE * LOG2E)
        put(qT_ref, i, 0, q.reshape(A_WIDTH, TSUB).astype(BF16))

    def seg_qb(i, h):
        q = head_norm(proj(h, OFF_QB, B_WIDTH), qnb_ref) * (SCALE * LOG2E)
        put(qT_ref, i, A_WIDTH, q.reshape(B_WIDTH, TSUB).astype(BF16))

    def seg_ka(i, h):
        kk = rope(head_norm(proj(h, OFF_KA, A_KV_WIDTH), kna_ref), i).reshape(A_KV_WIDTH, TSUB)
        k_ref[0, i * TSUB:(i + 1) * TSUB, 0:A_KV_WIDTH] = kk.T.astype(BF16)

    def seg_kb(i, h):
        kk = head_norm(proj(h, OFF_KB, B_WIDTH), knb_ref).reshape(B_WIDTH, TSUB)
        k_ref[0, i * TSUB:(i + 1) * TSUB, A_KV_WIDTH:KV_WIDTH] = kk.T.astype(BF16)

    def seg_va(i, h):
        put(vT_ref, i, 0, proj(h, OFF_VA, A_KV_WIDTH).astype(BF16))

    def seg_vb(i, h):
        put(vT_ref, i, A_KV_WIDTH, proj(h, OFF_VB, B_WIDTH).astype(BF16))

    segs = [seg_ga, seg_qa, seg_gb, seg_qb, seg_kb, seg_ka, seg_vb, seg_va]
    hs = [norm_h(0)] + [None] * (n_sub - 1)
    for i in range(n_sub):
        for j, seg in enumerate(segs):
            seg(i, hs[i])
            if j == 1 and i + 1 < n_sub:
                hs[i + 1] = norm_h(i + 1)


def _project(x, nw, w_in_t, qna, kna, qnb, knb, cos_t, sin_t):
    b = x.shape[0]
    const = lambda *shape: pl.BlockSpec(shape, lambda i, j: (0,) * len(shape))
    blk = TM // LANE
    return pl.pallas_call(
        _proj_kernel,
        grid=(b, SEQ // TM),
        in_specs=[
            pl.BlockSpec((1, TM, D_MODEL), lambda i, j: (i, j, 0)),
            const(1, D_MODEL),
            const(IN_WIDTH, D_MODEL),
            const(HEAD_DIM, 1), const(HEAD_DIM, 1), const(HEAD_DIM, 1), const(HEAD_DIM, 1),
            pl.BlockSpec((ROT_HALF, TM), lambda i, j: (0, j)),
            pl.BlockSpec((ROT_HALF, TM), lambda i, j: (0, j)),
        ],
        out_specs=[
            pl.BlockSpec((1, blk, MIX_WIDTH, LANE), lambda i, j: (i, j, 0, 0)),
            pl.BlockSpec((1, blk, KV_WIDTH, LANE), lambda i, j: (i, j, 0, 0)),
            pl.BlockSpec((1, blk, MIX_WIDTH, LANE), lambda i, j: (i, j, 0, 0)),
            pl.BlockSpec((1, TM, KV_WIDTH), lambda i, j: (i, j, 0)),
        ],
        out_shape=[
            jax.ShapeDtypeStruct((b, N_BLK, MIX_WIDTH, LANE), BF16),
            jax.ShapeDtypeStruct((b, N_BLK, KV_WIDTH, LANE), BF16),
            jax.ShapeDtypeStruct((b, N_BLK, MIX_WIDTH, LANE), BF16),
            jax.ShapeDtypeStruct((b, SEQ, KV_WIDTH), BF16),
        ],
        compiler_params=pltpu.CompilerParams(
            dimension_semantics=("arbitrary", "arbitrary"), vmem_limit_bytes=VMEM_LIMIT_BYTES),
        name="proj",
    )(x, nw, w_in_t, qna, kna, qnb, knb, cos_t, sin_t)


def _attn_kernel(idx_ref, qT_ref, sgT_ref, k_ref, vT_ref, x_ref, p_ref, wout_ref, wgate_ref, wple_ref,
                 mask_ref, sink_ref, bias_ref, y_ref, oT_ref, s_ref, pr_ref, og_ref, y1_ref, y1b_ref):
    t_idx = pl.program_id(1)
    nblk = TQ // LANE
    zq = jnp.zeros((HEAD_DIM, LANE), BF16)
    ones_rows = jnp.ones((BF16_SUBLANES, LANE), BF16)
    sub = LANE // 8

    units = []
    for jj in range(nblk):
        a_units = [("A", jj, g, half) for g in range(A_KV_HEADS) for half in range(2)]
        b_units = [("B", jj, hp, 0) for hp in range(B_HEADS // 2)]
        for a_u, b_u in zip(a_units, b_units):
            units += [a_u, b_u]
    state = [None] * len(units)

    def prep(u):
        kind, jj, i0, i1 = units[u]
        jb = t_idx * nblk + jj
        st = {"kind": kind, "jj": jj, "sbuf": u % (S_AHEAD + 1), "pbuf": u % 2}
        if kind == "A":
            g, half = i0, i1
            h0 = A_GROUP * g + 2 * half
            qcat = jnp.concatenate([qT_ref[0, jj, h0 * HEAD_DIM:(h0 + 1) * HEAD_DIM, :],
                                    qT_ref[0, jj, (h0 + 1) * HEAD_DIM:(h0 + 2) * HEAD_DIM, :]], axis=1)
            zc = jnp.zeros_like(qcat)
            st["rhs"] = jnp.concatenate([qcat, zc] if g == 0 else [zc, qcat], axis=0)
            st["n"] = A_KEY_BLKS
            st["kb0"] = jnp.clip(jb - 1, 0, N_BLK - A_KEY_BLKS)
            st["klanes"] = slice(0, A_KV_WIDTH)
            st["vrows"] = slice(g * HEAD_DIM, (g + 1) * HEAD_DIM)
            st["variant"] = jnp.where(jb == 0, 0, jnp.where(jb == N_BLK - 1, 2, 1))
            st["sink"] = sink_ref[2 * g + half]
            st["m"] = jnp.broadcast_to(st["sink"], (8, 2 * LANE))
            st["out"] = [(h0 * HEAD_DIM, slice(0, HEAD_DIM), slice(0, LANE)),
                         ((h0 + 1) * HEAD_DIM, slice(0, HEAD_DIM), slice(LANE, 2 * LANE))]
        else:
            hp = i0
            lo = A_KV_WIDTH + hp * LANE
            qlo = A_WIDTH + hp * LANE
            q0 = qT_ref[0, jj, qlo:qlo + HEAD_DIM, :]
            q1 = qT_ref[0, jj, qlo + HEAD_DIM:qlo + LANE, :]
            st["rhs"] = jnp.concatenate([jnp.concatenate([q0, zq], axis=1),
                                         jnp.concatenate([zq, q1], axis=1)], axis=0)
            st["n"] = B_KEY_BLKS
            st["kb0"] = jnp.clip(jb - 2, 0, N_BLK - B_KEY_BLKS)
            st["klanes"] = slice(lo, lo + LANE)
            st["vrows"] = slice(lo, lo + LANE)
            st["hp"] = hp
            st["jb"] = jb
            st["m"] = jnp.full((8, 2 * LANE), NEG_INF, F32)
            st["out"] = [(qlo, slice(0, HEAD_DIM), slice(0, LANE)),
                         (qlo + HEAD_DIM, slice(HEAD_DIM, LANE), slice(LANE, 2 * LANE))]
        state[u] = st

    def s_chunk(u, c):
        st = state[u]
        rows = pl.ds(pl.multiple_of((st["kb0"] + c) * LANE, LANE), LANE)
        t = jnp.dot(k_ref[0, rows, st["klanes"]], st["rhs"], preferred_element_type=F32)
        if st["kind"] == "A":
            mk = mask_ref[st["variant"], c]
            bias = jnp.concatenate([mk, mk], axis=1)
        else:
            e0 = idx_ref[st["jb"] * B_KEY_ROWS + 2 * c]
            e1 = idx_ref[st["jb"] * B_KEY_ROWS + 2 * c + 1]
            hp = st["hp"]
            bias = jnp.concatenate(
                [jnp.concatenate([bias_ref[e0, 2 * hp], bias_ref[e0, 2 * hp + 1]], axis=1),
                 jnp.concatenate([bias_ref[e1, 2 * hp], bias_ref[e1, 2 * hp + 1]], axis=1)], axis=0)
        t = t + bias
        s_ref[st["sbuf"], c * LANE:(c + 1) * LANE, :] = t
        st["m"] = jnp.maximum(st["m"], jnp.max(t.reshape(sub, 8, 2 * LANE), axis=0))

    def s_finish(u):
        st = state[u]
        st["m"] = jnp.broadcast_to(jnp.max(st["m"], axis=0, keepdims=True), (8, 2 * LANE))

    def e_chunk(u, c):
        st = state[u]
        t = s_ref[st["sbuf"], c * LANE:(c + 1) * LANE, :]
        p = jnp.exp2(t.reshape(sub, 8, 2 * LANE) - st["m"][None]).reshape(LANE, 2 * LANE)
        pr_ref[st["pbuf"], c * LANE:(c + 1) * LANE, :] = p.astype(BF16)
        if c % PV_CHUNKS == PV_CHUNKS - 1 or c == st["n"] - 1:
            c0 = c - c % PV_CHUNKS
            v = jnp.concatenate(
                [jnp.concatenate([vT_ref[0, st["kb0"] + i, st["vrows"], :], ones_rows], axis=0)
                 for i in range(c0, c + 1)], axis=1)
            part = jnp.dot(v, pr_ref[st["pbuf"], c0 * LANE:(c + 1) * LANE, :], preferred_element_type=F32)
            st["o"] = part if c0 == 0 else st["o"] + part

    def pv(u):
        st = state[u]
        o = st["o"]
        nv = o.shape[0] - BF16_SUBLANES
        den = o[nv:nv + 1, :]
        if st["kind"] == "A":
            den = den + jnp.exp2(st["sink"] - st["m"][0:1, :])
        r = 1.0 / den
        lanes = slice(st["jj"] * LANE, (st["jj"] + 1) * LANE)
        for row0, rsl, lsl in st["out"]:
            oT_ref[row0:row0 + HEAD_DIM, lanes] = o[rsl, lsl] * r[:, lsl]
        state[u] = None

    n_units = len(units)
    def out_transpose(jj):
        lanes = slice(jj * LANE, (jj + 1) * LANE)
        og = oT_ref[:, lanes] * sgT_ref[0, jj].astype(F32)
        og_ref[jj % 2] = og.T.astype(BF16)

    def out_proj(jj, c):
        rows = slice(jj * LANE, (jj + 1) * LANE)
        cols = slice(c * PROJ_N, (c + 1) * PROJ_N)
        y1 = x_ref[0, rows, cols] + jnp.dot(og_ref[jj % 2], wout_ref[:, cols], preferred_element_type=F32)
        y1_ref[jj % 2, :, cols] = y1
        y1b_ref[jj % 2, :, cols] = y1.astype(BF16)

    def gate_proj(jj, c):
        rows = slice(jj * LANE, (jj + 1) * LANE)
        cols = slice(c * PROJ_N, (c + 1) * PROJ_N)
        z = jnp.dot(y1b_ref[jj % 2], wgate_ref[:, cols], preferred_element_type=F32)
        ple = jnp.dot(p_ref[0, rows, :].astype(BF16), wple_ref[:, cols], preferred_element_type=F32)
        y_ref[0, rows, cols] = y1_ref[jj % 2, :, cols] + ple / (1.0 + jnp.exp(-z))

    n_proj = D_MODEL // PROJ_N
    units_per_blk = len(units) // nblk

    def proj_pieces(jj):
        pieces = [functools.partial(out_transpose, jj)]
        pieces += [functools.partial(out_proj, jj, c) for c in range(n_proj)]
        pieces += [functools.partial(gate_proj, jj, c) for c in range(n_proj)]
        return pieces

    pending = []
    for v in range(min(S_AHEAD, n_units)):
        prep(v)
        for c in range(state[v]["n"]):
            s_chunk(v, c)
        s_finish(v)
    for u in range(n_units):
        nxt = u + S_AHEAD if u + S_AHEAD < n_units else None
        if nxt is not None:
            prep(nxt)
        n_s = state[nxt]["n"] if nxt is not None else 0
        n_e = state[u]["n"]
        for c in range(max(n_s, n_e)):
            if c < n_s:
                s_chunk(nxt, c)
            if c < n_e:
                e_chunk(u, c)
        if nxt is not None:
            s_finish(nxt)
        pv(u)
        slots_left = units_per_blk - u % units_per_blk
        for _ in range(-(-len(pending) // slots_left)):
            pending.pop(0)()
        if (u + 1) % units_per_blk == 0:
            assert not pending
            pending = proj_pieces(u // units_per_blk)
    for piece in pending:
        piece()


def _attend(idx, qT, sgT, k, vT, x, p, w_out, w_gate, w_ple, mask_a, sink_rows, bias_tab):
    b = x.shape[0]
    const = lambda *shape: pl.BlockSpec(shape, lambda i, j, idx_ref: (0,) * len(shape))
    blk = TQ // LANE
    grid_spec = pltpu.PrefetchScalarGridSpec(
        num_scalar_prefetch=1,
        grid=(b, SEQ // TQ),
        in_specs=[
            pl.BlockSpec((1, blk, MIX_WIDTH, LANE), lambda i, j, idx_ref: (i, j, 0, 0)),
            pl.BlockSpec((1, blk, MIX_WIDTH, LANE), lambda i, j, idx_ref: (i, j, 0, 0)),
            pl.BlockSpec((1, SEQ, KV_WIDTH), lambda i, j, idx_ref: (i, 0, 0)),
            pl.BlockSpec((1, N_BLK, KV_WIDTH, LANE), lambda i, j, idx_ref: (i, 0, 0, 0)),
            pl.BlockSpec((1, TQ, D_MODEL), lambda i, j, idx_ref: (i, j, 0)),
            pl.BlockSpec((1, TQ, PLE_DIM), lambda i, j, idx_ref: (i, j, 0)),
            const(MIX_WIDTH, D_MODEL),
            const(D_MODEL, D_MODEL),
            const(PLE_DIM, D_MODEL),
            const(*mask_a.shape),
            const(*sink_rows.shape),
            const(*bias_tab.shape),
        ],
        out_specs=pl.BlockSpec((1, TQ, D_MODEL), lambda i, j, idx_ref: (i, j, 0)),
        scratch_shapes=[
            pltpu.VMEM((MIX_WIDTH, TQ), F32),
            pltpu.VMEM((S_AHEAD + 1, B_KEY_BLKS * LANE, 2 * LANE), F32),
            pltpu.VMEM((2, B_KEY_BLKS * LANE, 2 * LANE), BF16),
            pltpu.VMEM((2, LANE, MIX_WIDTH), BF16),
            pltpu.VMEM((2, LANE, D_MODEL), F32),
            pltpu.VMEM((2, LANE, D_MODEL), BF16),
        ],
    )
    return pl.pallas_call(
        _attn_kernel,
        grid_spec=grid_spec,
        out_shape=jax.ShapeDtypeStruct((b, SEQ, D_MODEL), F32),
        compiler_params=pltpu.CompilerParams(
            dimension_semantics=("arbitrary", "arbitrary"), vmem_limit_bytes=VMEM_LIMIT_BYTES),
        name="attn",
    )(idx, qT, sgT, k, vT, x, p, w_out, w_gate, w_ple, mask_a, sink_rows, bias_tab)


def _rope_tables():
    inv_freq = jnp.power(ROPE_THETA, -jnp.arange(0, ROT_DIM, 2, dtype=F32) / ROT_DIM)
    ang = jnp.arange(SEQ).astype(F32)[:, None] * inv_freq[None, :]
    return jnp.cos(ang).T, jnp.sin(ang).T


def _mask_a():
    i = np.arange(A_KEY_BLKS * LANE)[:, None]
    q = np.arange(LANE)[None, :]
    out = []
    for jb in (0, 1, N_BLK - 1):
        kb0 = min(max(jb - 1, 0), N_BLK - A_KEY_BLKS)
        diff = (kb0 * LANE + i) - (jb * LANE + q)
        out.append(np.where(np.abs(diff) <= WINDOW, 0.0, NEG_INF).reshape(A_KEY_BLKS, LANE, LANE))
    return np.stack(out).astype(np.float32)


def _bias_plan():
    entries = {}
    idx = np.zeros((GRID_ROWS // B_ROWS_PER_BLK, B_KEY_ROWS), np.int32)
    for rp in range(GRID_ROWS // B_ROWS_PER_BLK):
        r = B_ROWS_PER_BLK * rp
        kr0 = min(max(r - NA_ROWS // 2, 0), GRID_ROWS - B_KEY_ROWS)
        rs_l = min(max(r - NA_ROWS // 2, 0), GRID_ROWS - NA_ROWS)
        rs_r = min(max(r + 1 - NA_ROWS // 2, 0), GRID_ROWS - NA_ROWS)
        for a in range(B_KEY_ROWS):
            kr = kr0 + a
            v_l = rs_l <= kr < rs_l + NA_ROWS
            v_r = rs_r <= kr < rs_r + NA_ROWS
            d_l = kr - r + NA_ROWS - 1
            key = (bool(v_l), bool(v_r), d_l if (v_l or v_r) else 0)
            idx[rp, a] = entries.setdefault(key, len(entries))
    return list(entries.keys()), idx.reshape(-1)


def _bias_table(rpb, entries):
    ck = np.arange(GRID_W)[:, None]
    cq = np.arange(GRID_W)[None, :]
    cs = np.clip(cq - NA_COLS // 2, 0, GRID_W - NA_COLS)
    valid = (ck >= cs) & (ck < cs + NA_COLS)
    dcol = np.clip(ck - cq, -(NA_COLS - 1), NA_COLS - 1) + NA_COLS - 1
    tab = jnp.where(valid[None, None], rpb[:, :, dcol], NEG_INF)
    neg = jnp.full((B_HEADS, GRID_W, GRID_W), NEG_INF, F32)
    blocks = []
    for v_l, v_r, d_l in entries:
        left = tab[:, d_l] if v_l else neg
        right = tab[:, d_l - 1] if v_r else neg
        blocks.append(jnp.concatenate([left, right], axis=-1))
    return jnp.stack(blocks)


def _layer(x, p, consts):
    (nw, w_in_t, qna, kna, qnb, knb, cos_t, sin_t, idx, w_out, w_gate, w_ple, mask_a, sink_rows,
     bias_tab) = consts
    qT, vT, sgT, k = _project(x, nw, w_in_t, qna, kna, qnb, knb, cos_t, sin_t)
    return _attend(idx, qT, sgT, k, vT, x, p, w_out, w_gate, w_ple, mask_a, sink_rows, bias_tab)


def kernel(x_prompt, x_sample, p_prompt, p_sample, norm_w, w_in, q_norm_a, k_norm_a, sink_a,
           q_norm_b, k_norm_b, rpb_b, w_out, w_ple, w_ple_gate):
    assert x_prompt.shape[1:] == (SEQ, D_MODEL) and x_sample.shape[1:] == (SEQ, D_MODEL)
    assert norm_w.shape[0] == 1, "one layer"
    col = lambda v: v[0].reshape(HEAD_DIM, 1).astype(F32)
    cos_t, sin_t = _rope_tables()
    entries, idx = _bias_plan()
    consts = (
        norm_w[0].reshape(1, D_MODEL),
        w_in[0].T.astype(BF16),
        col(q_norm_a), col(k_norm_a), col(q_norm_b), col(k_norm_b),
        cos_t, sin_t,
        jnp.asarray(idx),
        w_out[0].astype(BF16), w_ple_gate[0].astype(BF16), w_ple[0].astype(BF16),
        jnp.asarray(_mask_a()),
        jnp.repeat(sink_a[0].astype(F32) * LOG2E, LANE).reshape(A_HEADS // 2, 1, 2 * LANE),
        _bias_table(rpb_b[0].astype(F32) * LOG2E, entries),
    )
    return (_layer(x_prompt, p_prompt[0], consts), _layer(x_sample, p_sample[0], consts))
```

```python
import functools

import numpy as np
import jax
import jax.numpy as jnp
from jax import lax
from jax.experimental import pallas as pl
from jax.experimental.pallas import tpu as pltpu

D_MODEL = 1024
SEQ = 2048
HEAD_DIM = 64
A_HEADS = 8
A_KV_HEADS = 2
A_GROUP = A_HEADS // A_KV_HEADS
B_HEADS = 8
A_WIDTH = A_HEADS * HEAD_DIM
A_KV_WIDTH = A_KV_HEADS * HEAD_DIM
B_WIDTH = B_HEADS * HEAD_DIM
MIX_WIDTH = A_WIDTH + B_WIDTH
KV_WIDTH = A_KV_WIDTH + B_WIDTH
WINDOW = 128
ROT_DIM = HEAD_DIM // 4
ROT_HALF = ROT_DIM // 2
ROPE_THETA = 500000.0
GRID_W = 64
GRID_ROWS = SEQ // GRID_W
NA_ROWS = 8
NA_COLS = 16
PLE_DIM = 256
EPS = 1e-6
SCALE = HEAD_DIM ** -0.5
LOG2E = float(np.log2(np.e))

OFF_QA = 0
OFF_KA = OFF_QA + A_WIDTH
OFF_VA = OFF_KA + A_KV_WIDTH
OFF_GA = OFF_VA + A_KV_WIDTH
OFF_QB = OFF_GA + A_WIDTH
OFF_KB = OFF_QB + B_WIDTH
OFF_VB = OFF_KB + B_WIDTH
OFF_GB = OFF_VB + B_WIDTH
IN_WIDTH = OFF_GB + B_WIDTH

LANE = 128
BF16_SUBLANES = 16
N_BLK = SEQ // LANE
A_KEY_BLKS = 3
B_ROWS_PER_BLK = LANE // GRID_W
B_KEY_ROWS = 10
B_KEY_BLKS = B_KEY_ROWS * GRID_W // LANE
TM = 1024
TSUB = 256
TQ = 512
PROJ_N = 256
PV_CHUNKS = 2
S_AHEAD = 2
VMEM_LIMIT_BYTES = 56 * 1024 * 1024

F32 = jnp.float32
BF16 = jnp.bfloat16
NEG_INF = float("-inf")


def _proj_kernel(x_ref, nw_ref, w_ref, qna_ref, kna_ref, qnb_ref, knb_ref, cos_ref, sin_ref,
                 qT_ref, vT_ref, sgT_ref, k_ref):
    n_sub = x_ref.shape[1] // TSUB

    def norm_ht(i):
        x = x_ref[0, i * TSUB:(i + 1) * TSUB, :]
        ms = jnp.mean(x * x, axis=-1, keepdims=True)
        return (x * lax.rsqrt(ms + EPS) * nw_ref[...]).T.astype(BF16)

    def head_norm(r, g_ref):
        r3 = r.reshape(r.shape[0] // HEAD_DIM, HEAD_DIM, r.shape[1])
        m2 = jnp.mean(r3 * r3, axis=1, keepdims=True)
        return r3 * lax.rsqrt(m2 + EPS) * g_ref[...][None]

    def rope(y, i):
        c = cos_ref[:, i * TSUB:(i + 1) * TSUB][None]
        s = sin_ref[:, i * TSUB:(i + 1) * TSUB][None]
        x1 = y[:, 0:ROT_HALF]
        x2 = y[:, ROT_HALF:ROT_DIM]
        return jnp.concatenate([x1 * c - x2 * s, x2 * c + x1 * s, y[:, ROT_DIM:]], axis=1)

    def silu(g):
        hg = 0.5 * g
        return hg + hg * jnp.tanh(hg)

    def put(ref, i, row0, val):
        for c in range(TSUB // LANE):
            ref[0, i * (TSUB // LANE) + c, row0:row0 + val.shape[0], :] = val[:, c * LANE:(c + 1) * LANE]

    def chunk(i, ht, kind, src, n, dst):
        r = jnp.dot(w_ref[src:src + n, :], ht, preferred_element_type=F32)
        if kind == "gate":
            put(sgT_ref, i, dst, silu(r).astype(BF16))
        elif kind == "v":
            put(vT_ref, i, dst, r.astype(BF16))
        elif kind == "qa":
            put(qT_ref, i, dst, rope(head_norm(r, qna_ref), i).reshape(n, TSUB).astype(BF16))
        elif kind == "qb":
            put(qT_ref, i, dst, head_norm(r, qnb_ref).reshape(n, TSUB).astype(BF16))
        else:
            kk = head_norm(r, kna_ref if kind == "ka" else knb_ref)
            kk = (rope(kk, i) if kind == "ka" else kk).reshape(n, TSUB)
            k_ref[0, i * TSUB:(i + 1) * TSUB, dst:dst + n] = kk.T.astype(BF16)

    half_a, half_b = A_WIDTH // 2, B_WIDTH // 2
    chunks = [
        ("gate", OFF_GA, half_a, 0), ("qa", OFF_QA, half_a, 0),
        ("gate", OFF_GA + half_a, half_a, half_a), ("qa", OFF_QA + half_a, half_a, half_a),
        ("gate", OFF_GB, half_b, A_WIDTH), ("qb", OFF_QB, half_b, A_WIDTH),
        ("gate", OFF_GB + half_b, half_b, A_WIDTH + half_b), ("qb", OFF_QB + half_b, half_b, A_WIDTH + half_b),
        ("ka", OFF_KA, A_KV_WIDTH, 0), ("v", OFF_VA, A_KV_WIDTH, 0),
        ("kb", OFF_KB, half_b, A_KV_WIDTH), ("kb", OFF_KB + half_b, half_b, A_KV_WIDTH + half_b),
        ("v", OFF_VB, half_b, A_KV_WIDTH), ("v", OFF_VB + half_b, half_b, A_KV_WIDTH + half_b),
    ]
    hts = [norm_ht(0)] + [None] * (n_sub - 1)
    for i in range(n_sub):
        for j, (kind, src, n, dst) in enumerate(chunks):
            chunk(i, hts[i], kind, src, n, dst)
            if j == 1 and i + 1 < n_sub:
                hts[i + 1] = norm_ht(i + 1)


def _project(x, nw, w_in_t, qna, kna, qnb, knb, cos_t, sin_t):
    b = x.shape[0]
    const = lambda *shape: pl.BlockSpec(shape, lambda i, j: (0,) * len(shape))
    blk = TM // LANE
    return pl.pallas_call(
        _proj_kernel,
        grid=(b, SEQ // TM),
        in_specs=[
            pl.BlockSpec((1, TM, D_MODEL), lambda i, j: (i, j, 0)),
            const(1, D_MODEL),
            const(IN_WIDTH, D_MODEL),
            const(HEAD_DIM, 1), const(HEAD_DIM, 1), const(HEAD_DIM, 1), const(HEAD_DIM, 1),
            pl.BlockSpec((ROT_HALF, TM), lambda i, j: (0, j)),
            pl.BlockSpec((ROT_HALF, TM), lambda i, j: (0, j)),
        ],
        out_specs=[
            pl.BlockSpec((1, blk, MIX_WIDTH, LANE), lambda i, j: (i, j, 0, 0)),
            pl.BlockSpec((1, blk, KV_WIDTH, LANE), lambda i, j: (i, j, 0, 0)),
            pl.BlockSpec((1, blk, MIX_WIDTH, LANE), lambda i, j: (i, j, 0, 0)),
            pl.BlockSpec((1, TM, KV_WIDTH), lambda i, j: (i, j, 0)),
        ],
        out_shape=[
            jax.ShapeDtypeStruct((b, N_BLK, MIX_WIDTH, LANE), BF16),
            jax.ShapeDtypeStruct((b, N_BLK, KV_WIDTH, LANE), BF16),
            jax.ShapeDtypeStruct((b, N_BLK, MIX_WIDTH, LANE), BF16),
            jax.ShapeDtypeStruct((b, SEQ, KV_WIDTH), BF16),
        ],
        compiler_params=pltpu.CompilerParams(
            dimension_semantics=("arbitrary", "arbitrary"), vmem_limit_bytes=VMEM_LIMIT_BYTES),
        name="proj",
    )(x, nw, w_in_t, qna, kna, qnb, knb, cos_t, sin_t)


def _attn_kernel(idx_ref, qT_ref, sgT_ref, k_ref, vT_ref, x_ref, p_ref, wout_ref, wgate_ref, wple_ref,
                 mask_ref, sink_ref, bias_ref, y_ref, oT_ref, s_ref, pr_ref, og_ref, y1_ref, y1b_ref):
    t_idx = pl.program_id(1)
    nblk = TQ // LANE
    zq = jnp.zeros((HEAD_DIM, LANE), BF16)
    ones_rows = jnp.ones((BF16_SUBLANES, LANE), BF16)
    sub = LANE // 8

    units = []
    for jj in range(nblk):
        a_units = [("A", jj, g, half) for g in range(A_KV_HEADS) for half in range(2)]
        b_units = [("B", jj, hp, 0) for hp in range(B_HEADS // 2)]
        for a_u, b_u in zip(a_units, b_units):
            units += [a_u, b_u]
    state = [None] * len(units)

    def prep(u):
        kind, jj, i0, i1 = units[u]
        jb = t_idx * nblk + jj
        st = {"kind": kind, "jj": jj, "sbuf": u % (S_AHEAD + 1), "pbuf": u % 2}
        if kind == "A":
            g, half = i0, i1
            h0 = A_GROUP * g + 2 * half
            qcat = jnp.concatenate([qT_ref[0, jj, h0 * HEAD_DIM:(h0 + 1) * HEAD_DIM, :],
                                    qT_ref[0, jj, (h0 + 1) * HEAD_DIM:(h0 + 2) * HEAD_DIM, :]], axis=1)
            zc = jnp.zeros_like(qcat)
            st["rhs"] = jnp.concatenate([qcat, zc] if g == 0 else [zc, qcat], axis=0)
            st["n"] = A_KEY_BLKS
            st["kb0"] = jnp.clip(jb - 1, 0, N_BLK - A_KEY_BLKS)
            st["klanes"] = slice(0, A_KV_WIDTH)
            st["vrows"] = slice(g * HEAD_DIM, (g + 1) * HEAD_DIM)
            st["variant"] = jnp.where(jb == 0, 0, jnp.where(jb == N_BLK - 1, 2, 1))
            st["sink"] = sink_ref[2 * g + half]
            st["m"] = jnp.broadcast_to(st["sink"], (8, 2 * LANE))
            st["out"] = [(h0 * HEAD_DIM, slice(0, HEAD_DIM), slice(0, LANE)),
                         ((h0 + 1) * HEAD_DIM, slice(0, HEAD_DIM), slice(LANE, 2 * LANE))]
        else:
            hp = i0
            lo = A_KV_WIDTH + hp * LANE
            qlo = A_WIDTH + hp * LANE
            q0 = qT_ref[0, jj, qlo:qlo + HEAD_DIM, :]
            q1 = qT_ref[0, jj, qlo + HEAD_DIM:qlo + LANE, :]
            st["rhs"] = jnp.concatenate([jnp.concatenate([q0, zq], axis=1),
                                         jnp.concatenate([zq, q1], axis=1)], axis=0)
            st["n"] = B_KEY_BLKS
            st["kb0"] = jnp.clip(jb - 2, 0, N_BLK - B_KEY_BLKS)
            st["klanes"] = slice(lo, lo + LANE)
            st["vrows"] = slice(lo, lo + LANE)
            st["hp"] = hp
            st["jb"] = jb
            st["m"] = jnp.full((8, 2 * LANE), NEG_INF, F32)
            st["out"] = [(qlo, slice(0, HEAD_DIM), slice(0, LANE)),
                         (qlo + HEAD_DIM, slice(HEAD_DIM, LANE), slice(LANE, 2 * LANE))]
        state[u] = st

    def s_chunk(u, c):
        st = state[u]
        rows = pl.ds(pl.multiple_of((st["kb0"] + c) * LANE, LANE), LANE)
        t = jnp.dot(k_ref[0, rows, st["klanes"]], st["rhs"], preferred_element_type=F32)
        if st["kind"] == "A":
            mk = mask_ref[st["variant"], c]
            bias = jnp.concatenate([mk, mk], axis=1)
        else:
            e0 = idx_ref[st["jb"] * B_KEY_ROWS + 2 * c]
            e1 = idx_ref[st["jb"] * B_KEY_ROWS + 2 * c + 1]
            hp = st["hp"]
            bias = jnp.concatenate(
                [jnp.concatenate([bias_ref[e0, 2 * hp], bias_ref[e0, 2 * hp + 1]], axis=1),
                 jnp.concatenate([bias_ref[e1, 2 * hp], bias_ref[e1, 2 * hp + 1]], axis=1)], axis=0)
        t = t + bias
        s_ref[st["sbuf"], c * LANE:(c + 1) * LANE, :] = t
        st["m"] = jnp.maximum(st["m"], jnp.max(t.reshape(sub, 8, 2 * LANE), axis=0))

    def s_finish(u):
        st = state[u]
        st["m"] = jnp.broadcast_to(jnp.max(st["m"], axis=0, keepdims=True), (8, 2 * LANE))

    def e_chunk(u, c):
        st = state[u]
        t = s_ref[st["sbuf"], c * LANE:(c + 1) * LANE, :]
        p = jnp.exp2(t.reshape(sub, 8, 2 * LANE) - st["m"][None]).reshape(LANE, 2 * LANE)
        pr_ref[st["pbuf"], c * LANE:(c + 1) * LANE, :] = p.astype(BF16)
        if c % PV_CHUNKS == PV_CHUNKS - 1 or c == st["n"] - 1:
            c0 = c - c % PV_CHUNKS
            v = jnp.concatenate(
                [jnp.concatenate([vT_ref[0, st["kb0"] + i, st["vrows"], :], ones_rows], axis=0)
                 for i in range(c0, c + 1)], axis=1)
            part = jnp.dot(v, pr_ref[st["pbuf"], c0 * LANE:(c + 1) * LANE, :], preferred_element_type=F32)
            st["o"] = part if c0 == 0 else st["o"] + part

    def pv(u):
        st = state[u]
        o = st["o"]
        nv = o.shape[0] - BF16_SUBLANES
        den = o[nv:nv + 1, :]
        if st["kind"] == "A":
            den = den + jnp.exp2(st["sink"] - st["m"][0:1, :])
        r = 1.0 / den
        lanes = slice(st["jj"] * LANE, (st["jj"] + 1) * LANE)
        for row0, rsl, lsl in st["out"]:
            oT_ref[row0:row0 + HEAD_DIM, lanes] = o[rsl, lsl] * r[:, lsl]
        state[u] = None

    n_units = len(units)
    def out_transpose(jj):
        lanes = slice(jj * LANE, (jj + 1) * LANE)
        og = oT_ref[:, lanes] * sgT_ref[0, jj].astype(F32)
        og_ref[jj % 2] = og.T.astype(BF16)

    def out_proj(jj, c):
        rows = slice(jj * LANE, (jj + 1) * LANE)
        cols = slice(c * PROJ_N, (c + 1) * PROJ_N)
        y1 = x_ref[0, rows, cols] + jnp.dot(og_ref[jj % 2], wout_ref[:, cols], preferred_element_type=F32)
        y1_ref[jj % 2, :, cols] = y1
        y1b_ref[jj % 2, :, cols] = y1.astype(BF16)

    def gate_proj(jj, c):
        rows = slice(jj * LANE, (jj + 1) * LANE)
        cols = slice(c * PROJ_N, (c + 1) * PROJ_N)
        z = jnp.dot(y1b_ref[jj % 2], wgate_ref[:, cols], preferred_element_type=F32)
        ple = jnp.dot(p_ref[0, rows, :].astype(BF16), wple_ref[:, cols], preferred_element_type=F32)
        hp = 0.5 * ple
        y_ref[0, rows, cols] = y1_ref[jj % 2, :, cols] + (hp + hp * jnp.tanh(0.5 * z))

    n_proj = D_MODEL // PROJ_N
    units_per_blk = len(units) // nblk

    def proj_pieces(jj):
        pieces = [functools.partial(out_transpose, jj)]
        pieces += [functools.partial(out_proj, jj, c) for c in range(n_proj)]
        pieces += [functools.partial(gate_proj, jj, c) for c in range(n_proj)]
        return pieces

    pending = []
    for v in range(min(S_AHEAD, n_units)):
        prep(v)
        for c in range(state[v]["n"]):
            s_chunk(v, c)
        s_finish(v)
    for u in range(n_units):
        nxt = u + S_AHEAD if u + S_AHEAD < n_units else None
        if nxt is not None:
            prep(nxt)
        n_s = state[nxt]["n"] if nxt is not None else 0
        n_e = state[u]["n"]
        for c in range(max(n_s, n_e)):
            if c < n_s:
                s_chunk(nxt, c)
            if c < n_e:
                e_chunk(u, c)
        if nxt is not None:
            s_finish(nxt)
        pv(u)
        slots_left = units_per_blk - u % units_per_blk
        for _ in range(-(-len(pending) // slots_left)):
            pending.pop(0)()
        if (u + 1) % units_per_blk == 0:
            assert not pending
            pending = proj_pieces(u // units_per_blk)
    for piece in pending:
        piece()


def _attend(idx, qT, sgT, k, vT, x, p, w_out, w_gate, w_ple, mask_a, sink_rows, bias_tab):
    b = x.shape[0]
    const = lambda *shape: pl.BlockSpec(shape, lambda i, j, idx_ref: (0,) * len(shape))
    blk = TQ // LANE
    grid_spec = pltpu.PrefetchScalarGridSpec(
        num_scalar_prefetch=1,
        grid=(b, SEQ // TQ),
        in_specs=[
            pl.BlockSpec((1, blk, MIX_WIDTH, LANE), lambda i, j, idx_ref: (i, j, 0, 0)),
            pl.BlockSpec((1, blk, MIX_WIDTH, LANE), lambda i, j, idx_ref: (i, j, 0, 0)),
            pl.BlockSpec((1, SEQ, KV_WIDTH), lambda i, j, idx_ref: (i, 0, 0)),
            pl.BlockSpec((1, N_BLK, KV_WIDTH, LANE), lambda i, j, idx_ref: (i, 0, 0, 0)),
            pl.BlockSpec((1, TQ, D_MODEL), lambda i, j, idx_ref: (i, j, 0)),
            pl.BlockSpec((1, TQ, PLE_DIM), lambda i, j, idx_ref: (i, j, 0)),
            const(MIX_WIDTH, D_MODEL),
            const(D_MODEL, D_MODEL),
            const(PLE_DIM, D_MODEL),
            const(*mask_a.shape),
            const(*sink_rows.shape),
            const(*bias_tab.shape),
        ],
        out_specs=pl.BlockSpec((1, TQ, D_MODEL), lambda i, j, idx_ref: (i, j, 0)),
        scratch_shapes=[
            pltpu.VMEM((MIX_WIDTH, TQ), F32),
            pltpu.VMEM((S_AHEAD + 1, B_KEY_BLKS * LANE, 2 * LANE), F32),
            pltpu.VMEM((2, B_KEY_BLKS * LANE, 2 * LANE), BF16),
            pltpu.VMEM((2, LANE, MIX_WIDTH), BF16),
            pltpu.VMEM((2, LANE, D_MODEL), F32),
            pltpu.VMEM((2, LANE, D_MODEL), BF16),
        ],
    )
    return pl.pallas_call(
        _attn_kernel,
        grid_spec=grid_spec,
        out_shape=jax.ShapeDtypeStruct((b, SEQ, D_MODEL), F32),
        compiler_params=pltpu.CompilerParams(
            dimension_semantics=("arbitrary", "arbitrary"), vmem_limit_bytes=VMEM_LIMIT_BYTES),
        name="attn",
    )(idx, qT, sgT, k, vT, x, p, w_out, w_gate, w_ple, mask_a, sink_rows, bias_tab)


def _rope_tables():
    inv_freq = jnp.power(ROPE_THETA, -jnp.arange(0, ROT_DIM, 2, dtype=F32) / ROT_DIM)
    ang = jnp.arange(SEQ).astype(F32)[:, None] * inv_freq[None, :]
    return jnp.cos(ang).T, jnp.sin(ang).T


def _mask_a():
    i = np.arange(A_KEY_BLKS * LANE)[:, None]
    q = np.arange(LANE)[None, :]
    out = []
    for jb in (0, 1, N_BLK - 1):
        kb0 = min(max(jb - 1, 0), N_BLK - A_KEY_BLKS)
        diff = (kb0 * LANE + i) - (jb * LANE + q)
        out.append(np.where(np.abs(diff) <= WINDOW, 0.0, NEG_INF).reshape(A_KEY_BLKS, LANE, LANE))
    return np.stack(out).astype(np.float32)


def _bias_plan():
    entries = {}
    idx = np.zeros((GRID_ROWS // B_ROWS_PER_BLK, B_KEY_ROWS), np.int32)
    for rp in range(GRID_ROWS // B_ROWS_PER_BLK):
        r = B_ROWS_PER_BLK * rp
        kr0 = min(max(r - NA_ROWS // 2, 0), GRID_ROWS - B_KEY_ROWS)
        rs_l = min(max(r - NA_ROWS // 2, 0), GRID_ROWS - NA_ROWS)
        rs_r = min(max(r + 1 - NA_ROWS // 2, 0), GRID_ROWS - NA_ROWS)
        for a in range(B_KEY_ROWS):
            kr = kr0 + a
            v_l = rs_l <= kr < rs_l + NA_ROWS
            v_r = rs_r <= kr < rs_r + NA_ROWS
            d_l = kr - r + NA_ROWS - 1
            key = (bool(v_l), bool(v_r), d_l if (v_l or v_r) else 0)
            idx[rp, a] = entries.setdefault(key, len(entries))
    return list(entries.keys()), idx.reshape(-1)


def _bias_table(rpb, entries):
    ck = np.arange(GRID_W)[:, None]
    cq = np.arange(GRID_W)[None, :]
    cs = np.clip(cq - NA_COLS // 2, 0, GRID_W - NA_COLS)
    valid = (ck >= cs) & (ck < cs + NA_COLS)
    dcol = np.clip(ck - cq, -(NA_COLS - 1), NA_COLS - 1) + NA_COLS - 1
    onehot = (dcol.reshape(-1)[None, :] == np.arange(2 * NA_COLS - 1)[:, None]).astype(np.float32)
    sel = jnp.einsum("hdk,kc->hdc", rpb, jnp.asarray(onehot), precision=lax.Precision.HIGHEST)
    tab = jnp.where(valid[None, None], sel.reshape(B_HEADS, -1, GRID_W, GRID_W), NEG_INF)
    neg = jnp.full((B_HEADS, GRID_W, GRID_W), NEG_INF, F32)
    blocks = []
    for v_l, v_r, d_l in entries:
        left = tab[:, d_l] if v_l else neg
        right = tab[:, d_l - 1] if v_r else neg
        blocks.append(jnp.concatenate([left, right], axis=-1))
    return jnp.stack(blocks)


def _layer(x, p, consts):
    (nw, w_in_t, qna, kna, qnb, knb, cos_t, sin_t, idx, w_out, w_gate, w_ple, mask_a, sink_rows,
     bias_tab) = consts
    qT, vT, sgT, k = _project(x, nw, w_in_t, qna, kna, qnb, knb, cos_t, sin_t)
    return _attend(idx, qT, sgT, k, vT, x, p, w_out, w_gate, w_ple, mask_a, sink_rows, bias_tab)


def kernel(x_prompt, x_sample, p_prompt, p_sample, norm_w, w_in, q_norm_a, k_norm_a, sink_a,
           q_norm_b, k_norm_b, rpb_b, w_out, w_ple, w_ple_gate):
    assert x_prompt.shape[1:] == (SEQ, D_MODEL) and x_sample.shape[1:] == (SEQ, D_MODEL)
    assert norm_w.shape[0] == 1, "one layer"
    col = lambda v: v[0].reshape(HEAD_DIM, 1).astype(F32)
    cos_t, sin_t = _rope_tables()
    entries, idx = _bias_plan()
    consts = (
        norm_w[0].reshape(1, D_MODEL),
        w_in[0].T.astype(BF16),
        col(q_norm_a) * (SCALE * LOG2E), col(k_norm_a), col(q_norm_b) * (SCALE * LOG2E), col(k_norm_b),
        cos_t, sin_t,
        jnp.asarray(idx),
        w_out[0].astype(BF16), w_ple_gate[0].astype(BF16), w_ple[0].astype(BF16),
        jnp.asarray(_mask_a()),
        jnp.repeat(sink_a[0].astype(F32) * LOG2E, LANE).reshape(A_HEADS // 2, 1, 2 * LANE),
        _bias_table(rpb_b[0].astype(F32) * LOG2E, entries),
    )
    return (_layer(x_prompt, p_prompt[0], consts), _layer(x_sample, p_sample[0], consts))
```

```python
import functools

import numpy as np
import jax
import jax.numpy as jnp
from jax import lax
from jax.experimental import pallas as pl
from jax.experimental.pallas import tpu as pltpu

D_MODEL = 1024
SEQ = 2048
HEAD_DIM = 64
A_HEADS = 8
A_KV_HEADS = 2
A_GROUP = A_HEADS // A_KV_HEADS
B_HEADS = 8
A_WIDTH = A_HEADS * HEAD_DIM
A_KV_WIDTH = A_KV_HEADS * HEAD_DIM
B_WIDTH = B_HEADS * HEAD_DIM
MIX_WIDTH = A_WIDTH + B_WIDTH
KV_WIDTH = A_KV_WIDTH + B_WIDTH
WINDOW = 128
ROT_DIM = HEAD_DIM // 4
ROT_HALF = ROT_DIM // 2
ROPE_THETA = 500000.0
GRID_W = 64
GRID_ROWS = SEQ // GRID_W
NA_ROWS = 8
NA_COLS = 16
PLE_DIM = 256
EPS = 1e-6
SCALE = HEAD_DIM ** -0.5
LOG2E = float(np.log2(np.e))

OFF_QA = 0
OFF_KA = OFF_QA + A_WIDTH
OFF_VA = OFF_KA + A_KV_WIDTH
OFF_GA = OFF_VA + A_KV_WIDTH
OFF_QB = OFF_GA + A_WIDTH
OFF_KB = OFF_QB + B_WIDTH
OFF_VB = OFF_KB + B_WIDTH
OFF_GB = OFF_VB + B_WIDTH
IN_WIDTH = OFF_GB + B_WIDTH

LANE = 128
BF16_SUBLANES = 16
N_BLK = SEQ // LANE
A_KEY_BLKS = 3
B_ROWS_PER_BLK = LANE // GRID_W
B_KEY_ROWS = 10
B_KEY_BLKS = B_KEY_ROWS * GRID_W // LANE
TM = 1024
TSUB = 256
TQ = 512
PROJ_N = 256
PROJ_M = 256
PV_CHUNKS = 2
S_AHEAD = 2
VMEM_LIMIT_BYTES = 56 * 1024 * 1024

F32 = jnp.float32
BF16 = jnp.bfloat16
NEG_INF = float("-inf")


def _proj_kernel(x_ref, nw_ref, w_ref, qna_ref, kna_ref, qnb_ref, knb_ref, cos_ref, sin_ref,
                 qT_ref, vT_ref, sgT_ref, k_ref):
    n_sub = x_ref.shape[1] // TSUB

    def norm_ht(i):
        x = x_ref[0, i * TSUB:(i + 1) * TSUB, :]
        ms = jnp.mean(x * x, axis=-1, keepdims=True)
        return (x * lax.rsqrt(ms + EPS) * nw_ref[...]).T.astype(BF16)

    def head_norm(r, g_ref):
        r3 = r.reshape(r.shape[0] // HEAD_DIM, HEAD_DIM, r.shape[1])
        m2 = jnp.mean(r3 * r3, axis=1, keepdims=True)
        return r3 * lax.rsqrt(m2 + EPS) * g_ref[...][None]

    def rope(y, i):
        c = cos_ref[:, i * TSUB:(i + 1) * TSUB][None]
        s = sin_ref[:, i * TSUB:(i + 1) * TSUB][None]
        x1 = y[:, 0:ROT_HALF]
        x2 = y[:, ROT_HALF:ROT_DIM]
        return jnp.concatenate([x1 * c - x2 * s, x2 * c + x1 * s, y[:, ROT_DIM:]], axis=1)

    def silu(g):
        hg = 0.5 * g
        return hg + hg * jnp.tanh(hg)

    def put(ref, i, row0, val):
        for c in range(TSUB // LANE):
            ref[0, i * (TSUB // LANE) + c, row0:row0 + val.shape[0], :] = val[:, c * LANE:(c + 1) * LANE]

    def chunk(i, ht, kind, src, n, dst):
        r = jnp.dot(w_ref[src:src + n, :], ht, preferred_element_type=F32)
        if kind == "gate":
            put(sgT_ref, i, dst, silu(r).astype(BF16))
        elif kind == "v":
            put(vT_ref, i, dst, r.astype(BF16))
        elif kind == "qa":
            put(qT_ref, i, dst, rope(head_norm(r, qna_ref), i).reshape(n, TSUB).astype(BF16))
        elif kind == "qb":
            put(qT_ref, i, dst, head_norm(r, qnb_ref).reshape(n, TSUB).astype(BF16))
        else:
            kk = head_norm(r, kna_ref if kind == "ka" else knb_ref)
            kk = (rope(kk, i) if kind == "ka" else kk).reshape(n, TSUB)
            k_ref[0, i * TSUB:(i + 1) * TSUB, dst:dst + n] = kk.T.astype(BF16)

    half_a, half_b = A_WIDTH // 2, B_WIDTH // 2
    chunks = [
        ("gate", OFF_GA, half_a, 0), ("qa", OFF_QA, half_a, 0),
        ("gate", OFF_GA + half_a, half_a, half_a), ("qa", OFF_QA + half_a, half_a, half_a),
        ("gate", OFF_GB, half_b, A_WIDTH), ("qb", OFF_QB, half_b, A_WIDTH),
        ("gate", OFF_GB + half_b, half_b, A_WIDTH + half_b), ("qb", OFF_QB + half_b, half_b, A_WIDTH + half_b),
        ("ka", OFF_KA, A_KV_WIDTH, 0), ("v", OFF_VA, A_KV_WIDTH, 0),
        ("kb", OFF_KB, half_b, A_KV_WIDTH), ("kb", OFF_KB + half_b, half_b, A_KV_WIDTH + half_b),
        ("v", OFF_VB, half_b, A_KV_WIDTH), ("v", OFF_VB + half_b, half_b, A_KV_WIDTH + half_b),
    ]
    hts = [norm_ht(0)] + [None] * (n_sub - 1)
    for i in range(n_sub):
        for j, (kind, src, n, dst) in enumerate(chunks):
            chunk(i, hts[i], kind, src, n, dst)
            if j == 1 and i + 1 < n_sub:
                hts[i + 1] = norm_ht(i + 1)


def _project(x, nw, w_in_t, qna, kna, qnb, knb, cos_t, sin_t):
    b = x.shape[0]
    const = lambda *shape: pl.BlockSpec(shape, lambda i, j: (0,) * len(shape))
    blk = TM // LANE
    return pl.pallas_call(
        _proj_kernel,
        grid=(b, SEQ // TM),
        in_specs=[
            pl.BlockSpec((1, TM, D_MODEL), lambda i, j: (i, j, 0)),
            const(1, D_MODEL),
            const(IN_WIDTH, D_MODEL),
            const(HEAD_DIM, 1), const(HEAD_DIM, 1), const(HEAD_DIM, 1), const(HEAD_DIM, 1),
            pl.BlockSpec((ROT_HALF, TM), lambda i, j: (0, j)),
            pl.BlockSpec((ROT_HALF, TM), lambda i, j: (0, j)),
        ],
        out_specs=[
            pl.BlockSpec((1, blk, MIX_WIDTH, LANE), lambda i, j: (i, j, 0, 0)),
            pl.BlockSpec((1, blk, KV_WIDTH, LANE), lambda i, j: (i, j, 0, 0)),
            pl.BlockSpec((1, blk, MIX_WIDTH, LANE), lambda i, j: (i, j, 0, 0)),
            pl.BlockSpec((1, TM, KV_WIDTH), lambda i, j: (i, j, 0)),
        ],
        out_shape=[
            jax.ShapeDtypeStruct((b, N_BLK, MIX_WIDTH, LANE), BF16),
            jax.ShapeDtypeStruct((b, N_BLK, KV_WIDTH, LANE), BF16),
            jax.ShapeDtypeStruct((b, N_BLK, MIX_WIDTH, LANE), BF16),
            jax.ShapeDtypeStruct((b, SEQ, KV_WIDTH), BF16),
        ],
        compiler_params=pltpu.CompilerParams(
            dimension_semantics=("arbitrary", "arbitrary"), vmem_limit_bytes=VMEM_LIMIT_BYTES),
        name="proj",
    )(x, nw, w_in_t, qna, kna, qnb, knb, cos_t, sin_t)


def _attn_kernel(idx_ref, qT_ref, sgT_ref, k_ref, vT_ref, x_ref, p_ref, wout_ref, wgate_ref, wple_ref,
                 mask_ref, sink_ref, bias_ref, y_ref, oT_ref, s_ref, pr_ref, og_ref, y1_ref, y1b_ref):
    step = pl.program_id(0)
    t_idx = lax.rem(jnp.minimum(step, pl.num_programs(0) - 2), SEQ // TQ)
    cur = lax.rem(step, 2)
    prev = 1 - cur
    nblk = TQ // LANE

    @pl.when(step == 0)
    def _():
        oT_ref[1] = jnp.zeros(oT_ref.shape[1:], F32)
    zq = jnp.zeros((HEAD_DIM, LANE), BF16)
    ones_rows = jnp.ones((BF16_SUBLANES, LANE), BF16)
    sub = LANE // 8

    units = []
    for jj in range(nblk):
        a_units = [("A", jj, g, half) for g in range(A_KV_HEADS) for half in range(2)]
        b_units = [("B", jj, hp, 0) for hp in range(B_HEADS // 2)]
        for a_u, b_u in zip(a_units, b_units):
            units += [a_u, b_u]
    state = [None] * len(units)

    def prep(u):
        kind, jj, i0, i1 = units[u]
        jb = t_idx * nblk + jj
        st = {"kind": kind, "jj": jj, "sbuf": u % (S_AHEAD + 1), "pbuf": u % 2}
        if kind == "A":
            g, half = i0, i1
            h0 = A_GROUP * g + 2 * half
            qcat = jnp.concatenate([qT_ref[0, jj, h0 * HEAD_DIM:(h0 + 1) * HEAD_DIM, :],
                                    qT_ref[0, jj, (h0 + 1) * HEAD_DIM:(h0 + 2) * HEAD_DIM, :]], axis=1)
            zc = jnp.zeros_like(qcat)
            st["rhs"] = jnp.concatenate([qcat, zc] if g == 0 else [zc, qcat], axis=0)
            st["n"] = A_KEY_BLKS
            st["kb0"] = jnp.clip(jb - 1, 0, N_BLK - A_KEY_BLKS)
            st["klanes"] = slice(0, A_KV_WIDTH)
            st["vrows"] = slice(g * HEAD_DIM, (g + 1) * HEAD_DIM)
            st["variant"] = jnp.where(jb == 0, 0, jnp.where(jb == N_BLK - 1, 2, 1))
            st["sink"] = sink_ref[2 * g + half]
            st["m"] = jnp.broadcast_to(st["sink"], (8, 2 * LANE))
            st["out"] = [(h0 * HEAD_DIM, slice(0, HEAD_DIM), slice(0, LANE)),
                         ((h0 + 1) * HEAD_DIM, slice(0, HEAD_DIM), slice(LANE, 2 * LANE))]
        else:
            hp = i0
            lo = A_KV_WIDTH + hp * LANE
            qlo = A_WIDTH + hp * LANE
            q0 = qT_ref[0, jj, qlo:qlo + HEAD_DIM, :]
            q1 = qT_ref[0, jj, qlo + HEAD_DIM:qlo + LANE, :]
            st["rhs"] = jnp.concatenate([jnp.concatenate([q0, zq], axis=1),
                                         jnp.concatenate([zq, q1], axis=1)], axis=0)
            st["n"] = B_KEY_BLKS
            st["kb0"] = jnp.clip(jb - 2, 0, N_BLK - B_KEY_BLKS)
            st["klanes"] = slice(lo, lo + LANE)
            st["vrows"] = slice(lo, lo + LANE)
            st["hp"] = hp
            st["jb"] = jb
            st["m"] = jnp.full((8, 2 * LANE), NEG_INF, F32)
            st["out"] = [(qlo, slice(0, HEAD_DIM), slice(0, LANE)),
                         (qlo + HEAD_DIM, slice(HEAD_DIM, LANE), slice(LANE, 2 * LANE))]
        state[u] = st

    def s_chunk(u, c):
        st = state[u]
        if c == 0:
            rows = pl.ds(pl.multiple_of(st["kb0"] * LANE, LANE), st["n"] * LANE)
            st["t"] = jnp.dot(k_ref[0, rows, st["klanes"]], st["rhs"], preferred_element_type=F32)
        t = st["t"][c * LANE:(c + 1) * LANE, :]
        if st["kind"] == "A":
            mk = mask_ref[st["variant"], c]
            bias = jnp.concatenate([mk, mk], axis=1)
        else:
            e0 = idx_ref[st["jb"] * B_KEY_ROWS + 2 * c]
            e1 = idx_ref[st["jb"] * B_KEY_ROWS + 2 * c + 1]
            hp = st["hp"]
            bias = jnp.concatenate(
                [jnp.concatenate([bias_ref[e0, 2 * hp], bias_ref[e0, 2 * hp + 1]], axis=1),
                 jnp.concatenate([bias_ref[e1, 2 * hp], bias_ref[e1, 2 * hp + 1]], axis=1)], axis=0)
        t = t + bias
        s_ref[st["sbuf"], c * LANE:(c + 1) * LANE, :] = t
        st["m"] = jnp.maximum(st["m"], jnp.max(t.reshape(sub, 8, 2 * LANE), axis=0))

    def s_finish(u):
        st = state[u]
        st["m"] = jnp.broadcast_to(jnp.max(st["m"], axis=0, keepdims=True), (8, 2 * LANE))

    def e_chunk(u, c):
        st = state[u]
        t = s_ref[st["sbuf"], c * LANE:(c + 1) * LANE, :]
        p = jnp.exp2(t.reshape(sub, 8, 2 * LANE) - st["m"][None]).reshape(LANE, 2 * LANE)
        pr_ref[st["pbuf"], c * LANE:(c + 1) * LANE, :] = p.astype(BF16)
        if c % PV_CHUNKS == PV_CHUNKS - 1 or c == st["n"] - 1:
            c0 = c - c % PV_CHUNKS
            v = jnp.concatenate(
                [jnp.concatenate([vT_ref[0, st["kb0"] + i, st["vrows"], :], ones_rows], axis=0)
                 for i in range(c0, c + 1)], axis=1)
            part = jnp.dot(v, pr_ref[st["pbuf"], c0 * LANE:(c + 1) * LANE, :], preferred_element_type=F32)
            st["o"] = part if c0 == 0 else st["o"] + part

    def pv(u):
        st = state[u]
        o = st["o"]
        nv = o.shape[0] - BF16_SUBLANES
        den = o[nv:nv + 1, :]
        if st["kind"] == "A":
            den = den + jnp.exp2(st["sink"] - st["m"][0:1, :])
        r = 1.0 / den
        lanes = slice(st["jj"] * LANE, (st["jj"] + 1) * LANE)
        for row0, rsl, lsl in st["out"]:
            oT_ref[cur, row0:row0 + HEAD_DIM, lanes] = o[rsl, lsl] * r[:, lsl]
        state[u] = None

    n_units = len(units)
    def out_transpose(jj):
        lanes = slice(jj * LANE, (jj + 1) * LANE)
        og = oT_ref[prev, :, lanes] * sgT_ref[0, jj].astype(F32)
        og_ref[lanes, :] = og.T.astype(BF16)

    def out_proj(rb, c):
        rows = slice(rb * PROJ_M, (rb + 1) * PROJ_M)
        cols = slice(c * PROJ_N, (c + 1) * PROJ_N)
        y1 = x_ref[0, rows, cols] + jnp.dot(og_ref[rows, :], wout_ref[:, cols], preferred_element_type=F32)
        y1_ref[rows, cols] = y1
        y1b_ref[rows, cols] = y1.astype(BF16)

    def gate_proj(rb, c):
        rows = slice(rb * PROJ_M, (rb + 1) * PROJ_M)
        cols = slice(c * PROJ_N, (c + 1) * PROJ_N)
        z = jnp.dot(y1b_ref[rows, :], wgate_ref[:, cols], preferred_element_type=F32)
        ple = jnp.dot(p_ref[0, rows, :].astype(BF16), wple_ref[:, cols], preferred_element_type=F32)
        hp = 0.5 * ple
        y_ref[0, rows, cols] = y1_ref[rows, cols] + (hp + hp * jnp.tanh(0.5 * z))

    n_proj = D_MODEL // PROJ_N
    pending = []
    for rb in range(TQ // PROJ_M):
        pending += [functools.partial(out_proj, rb, c) for c in range(n_proj)]
        pending += [functools.partial(gate_proj, rb, c) for c in range(n_proj)]

    for jj in range(nblk):
        out_transpose(jj)
    for v in range(min(S_AHEAD, n_units)):
        prep(v)
        for c in range(state[v]["n"]):
            s_chunk(v, c)
        s_finish(v)
        pending.pop(0)()
    n_pieces = len(pending)
    for u in range(n_units):
        nxt = u + S_AHEAD if u + S_AHEAD < n_units else None
        if nxt is not None:
            prep(nxt)
        n_s = state[nxt]["n"] if nxt is not None else 0
        n_e = state[u]["n"]
        for c in range(max(n_s, n_e)):
            if c < n_s:
                s_chunk(nxt, c)
            if c < n_e:
                e_chunk(u, c)
        if nxt is not None:
            s_finish(nxt)
        pv(u)
        while n_pieces - len(pending) < (u + 1) * n_pieces // n_units:
            pending.pop(0)()
    assert not pending


def _attend(idx, qT, sgT, k, vT, x, p, w_out, w_gate, w_ple, mask_a, sink_rows, bias_tab):
    b = x.shape[0]
    const = lambda *shape: pl.BlockSpec(shape, lambda s, idx_ref: (0,) * len(shape))
    blk = TQ // LANE
    nt = SEQ // TQ
    n_tiles = b * nt
    att = lambda s: jnp.minimum(s, n_tiles - 1)
    prj = lambda s: jnp.maximum(s - 1, 0)
    grid_spec = pltpu.PrefetchScalarGridSpec(
        num_scalar_prefetch=1,
        grid=(n_tiles + 1,),
        in_specs=[
            pl.BlockSpec((1, blk, MIX_WIDTH, LANE), lambda s, idx_ref: (att(s) // nt, att(s) % nt, 0, 0)),
            pl.BlockSpec((1, blk, MIX_WIDTH, LANE), lambda s, idx_ref: (prj(s) // nt, prj(s) % nt, 0, 0)),
            pl.BlockSpec((1, SEQ, KV_WIDTH), lambda s, idx_ref: (att(s) // nt, 0, 0)),
            pl.BlockSpec((1, N_BLK, KV_WIDTH, LANE), lambda s, idx_ref: (att(s) // nt, 0, 0, 0)),
            pl.BlockSpec((1, TQ, D_MODEL), lambda s, idx_ref: (prj(s) // nt, prj(s) % nt, 0)),
            pl.BlockSpec((1, TQ, PLE_DIM), lambda s, idx_ref: (prj(s) // nt, prj(s) % nt, 0)),
            const(MIX_WIDTH, D_MODEL),
            const(D_MODEL, D_MODEL),
            const(PLE_DIM, D_MODEL),
            const(*mask_a.shape),
            const(*sink_rows.shape),
            const(*bias_tab.shape),
        ],
        out_specs=pl.BlockSpec((1, TQ, D_MODEL), lambda s, idx_ref: (prj(s) // nt, prj(s) % nt, 0)),
        scratch_shapes=[
            pltpu.VMEM((2, MIX_WIDTH, TQ), F32),
            pltpu.VMEM((S_AHEAD + 1, B_KEY_BLKS * LANE, 2 * LANE), F32),
            pltpu.VMEM((2, B_KEY_BLKS * LANE, 2 * LANE), BF16),
            pltpu.VMEM((TQ, MIX_WIDTH), BF16),
            pltpu.VMEM((TQ, D_MODEL), F32),
            pltpu.VMEM((TQ, D_MODEL), BF16),
        ],
    )
    return pl.pallas_call(
        _attn_kernel,
        grid_spec=grid_spec,
        out_shape=jax.ShapeDtypeStruct((b, SEQ, D_MODEL), F32),
        compiler_params=pltpu.CompilerParams(
            dimension_semantics=("arbitrary",), vmem_limit_bytes=VMEM_LIMIT_BYTES),
        name="attn",
    )(idx, qT, sgT, k, vT, x, p, w_out, w_gate, w_ple, mask_a, sink_rows, bias_tab)


def _rope_tables():
    inv_freq = jnp.power(ROPE_THETA, -jnp.arange(0, ROT_DIM, 2, dtype=F32) / ROT_DIM)
    ang = jnp.arange(SEQ).astype(F32)[:, None] * inv_freq[None, :]
    return jnp.cos(ang).T, jnp.sin(ang).T


def _mask_a():
    i = np.arange(A_KEY_BLKS * LANE)[:, None]
    q = np.arange(LANE)[None, :]
    out = []
    for jb in (0, 1, N_BLK - 1):
        kb0 = min(max(jb - 1, 0), N_BLK - A_KEY_BLKS)
        diff = (kb0 * LANE + i) - (jb * LANE + q)
        out.append(np.where(np.abs(diff) <= WINDOW, 0.0, NEG_INF).reshape(A_KEY_BLKS, LANE, LANE))
    return np.stack(out).astype(np.float32)


def _bias_plan():
    entries = {}
    idx = np.zeros((GRID_ROWS // B_ROWS_PER_BLK, B_KEY_ROWS), np.int32)
    for rp in range(GRID_ROWS // B_ROWS_PER_BLK):
        r = B_ROWS_PER_BLK * rp
        kr0 = min(max(r - NA_ROWS // 2, 0), GRID_ROWS - B_KEY_ROWS)
        rs_l = min(max(r - NA_ROWS // 2, 0), GRID_ROWS - NA_ROWS)
        rs_r = min(max(r + 1 - NA_ROWS // 2, 0), GRID_ROWS - NA_ROWS)
        for a in range(B_KEY_ROWS):
            kr = kr0 + a
            v_l = rs_l <= kr < rs_l + NA_ROWS
            v_r = rs_r <= kr < rs_r + NA_ROWS
            d_l = kr - r + NA_ROWS - 1
            key = (bool(v_l), bool(v_r), d_l if (v_l or v_r) else 0)
            idx[rp, a] = entries.setdefault(key, len(entries))
    return list(entries.keys()), idx.reshape(-1)


def _bias_table(rpb, entries):
    ck = np.arange(GRID_W)[:, None]
    cq = np.arange(GRID_W)[None, :]
    cs = np.clip(cq - NA_COLS // 2, 0, GRID_W - NA_COLS)
    valid = (ck >= cs) & (ck < cs + NA_COLS)
    dcol = np.clip(ck - cq, -(NA_COLS - 1), NA_COLS - 1) + NA_COLS - 1
    onehot = (dcol.reshape(-1)[None, :] == np.arange(2 * NA_COLS - 1)[:, None]).astype(np.float32)
    sel = jnp.einsum("hdk,kc->hdc", rpb, jnp.asarray(onehot), precision=lax.Precision.HIGHEST)
    tab = jnp.where(valid[None, None], sel.reshape(B_HEADS, -1, GRID_W, GRID_W), NEG_INF)
    neg = jnp.full((B_HEADS, GRID_W, GRID_W), NEG_INF, F32)
    blocks = []
    for v_l, v_r, d_l in entries:
        left = tab[:, d_l] if v_l else neg
        right = tab[:, d_l - 1] if v_r else neg
        blocks.append(jnp.concatenate([left, right], axis=-1))
    return jnp.stack(blocks)


def _layer(x, p, consts):
    (nw, w_in_t, qna, kna, qnb, knb, cos_t, sin_t, idx, w_out, w_gate, w_ple, mask_a, sink_rows,
     bias_tab) = consts
    qT, vT, sgT, k = _project(x, nw, w_in_t, qna, kna, qnb, knb, cos_t, sin_t)
    return _attend(idx, qT, sgT, k, vT, x, p, w_out, w_gate, w_ple, mask_a, sink_rows, bias_tab)


def kernel(x_prompt, x_sample, p_prompt, p_sample, norm_w, w_in, q_norm_a, k_norm_a, sink_a,
           q_norm_b, k_norm_b, rpb_b, w_out, w_ple, w_ple_gate):
    assert x_prompt.shape[1:] == (SEQ, D_MODEL) and x_sample.shape[1:] == (SEQ, D_MODEL)
    assert norm_w.shape[0] == 1, "one layer"
    col = lambda v: v[0].reshape(HEAD_DIM, 1).astype(F32)
    cos_t, sin_t = _rope_tables()
    entries, idx = _bias_plan()
    consts = (
        norm_w[0].reshape(1, D_MODEL),
        w_in[0].T.astype(BF16),
        col(q_norm_a) * (SCALE * LOG2E), col(k_norm_a), col(q_norm_b) * (SCALE * LOG2E), col(k_norm_b),
        cos_t, sin_t,
        jnp.asarray(idx),
        w_out[0].astype(BF16), w_ple_gate[0].astype(BF16), w_ple[0].astype(BF16),
        jnp.asarray(_mask_a()),
        jnp.repeat(sink_a[0].astype(F32) * LOG2E, LANE).reshape(A_HEADS // 2, 1, 2 * LANE),
        _bias_table(rpb_b[0].astype(F32) * LOG2E, entries),
    )
    return (_layer(x_prompt, p_prompt[0], consts), _layer(x_sample, p_sample[0], consts))
```

```python
import functools

import numpy as np
import jax
import jax.numpy as jnp
from jax import lax
from jax.experimental import pallas as pl
from jax.experimental.pallas import tpu as pltpu

D_MODEL = 1024
SEQ = 2048
HEAD_DIM = 64
A_HEADS = 8
A_KV_HEADS = 2
A_GROUP = A_HEADS // A_KV_HEADS
B_HEADS = 8
A_WIDTH = A_HEADS * HEAD_DIM
A_KV_WIDTH = A_KV_HEADS * HEAD_DIM
B_WIDTH = B_HEADS * HEAD_DIM
MIX_WIDTH = A_WIDTH + B_WIDTH
KV_WIDTH = A_KV_WIDTH + B_WIDTH
WINDOW = 128
ROT_DIM = HEAD_DIM // 4
ROT_HALF = ROT_DIM // 2
ROPE_THETA = 500000.0
GRID_W = 64
GRID_ROWS = SEQ // GRID_W
NA_ROWS = 8
NA_COLS = 16
PLE_DIM = 256
EPS = 1e-6
SCALE = HEAD_DIM ** -0.5
LOG2E = float(np.log2(np.e))

OFF_QA = 0
OFF_KA = OFF_QA + A_WIDTH
OFF_VA = OFF_KA + A_KV_WIDTH
OFF_GA = OFF_VA + A_KV_WIDTH
OFF_QB = OFF_GA + A_WIDTH
OFF_KB = OFF_QB + B_WIDTH
OFF_VB = OFF_KB + B_WIDTH
OFF_GB = OFF_VB + B_WIDTH
IN_WIDTH = OFF_GB + B_WIDTH

LANE = 128
BF16_SUBLANES = 16
N_BLK = SEQ // LANE
A_KEY_BLKS = 3
B_ROWS_PER_BLK = LANE // GRID_W
B_KEY_ROWS = 10
B_KEY_BLKS = B_KEY_ROWS * GRID_W // LANE
TM = 1024
TSUB = 256
TQ = 512
PROJ_N = 256
PROJ_M = 256
PV_CHUNKS = 2
S_AHEAD = 2
VMEM_LIMIT_BYTES = 56 * 1024 * 1024

F32 = jnp.float32
BF16 = jnp.bfloat16
NEG_INF = float("-inf")


def _proj_kernel(x_ref, nw_ref, w_ref, qna_ref, kna_ref, qnb_ref, knb_ref, cos_ref, sin_ref,
                 qT_ref, vT_ref, sgT_ref, k_ref):
    n_sub = x_ref.shape[1] // TSUB

    def norm_ht(i):
        x = x_ref[0, i * TSUB:(i + 1) * TSUB, :]
        ms = jnp.mean(x * x, axis=-1, keepdims=True)
        return (x * lax.rsqrt(ms + EPS) * nw_ref[...]).T.astype(BF16)

    def head_norm(r, g_ref):
        r3 = r.reshape(r.shape[0] // HEAD_DIM, HEAD_DIM, r.shape[1])
        m2 = jnp.mean(r3 * r3, axis=1, keepdims=True)
        return r3 * lax.rsqrt(m2 + EPS) * g_ref[...][None]

    def rope(y, i):
        c = cos_ref[:, i * TSUB:(i + 1) * TSUB][None]
        s = sin_ref[:, i * TSUB:(i + 1) * TSUB][None]
        x1 = y[:, 0:ROT_HALF]
        x2 = y[:, ROT_HALF:ROT_DIM]
        return jnp.concatenate([x1 * c - x2 * s, x2 * c + x1 * s, y[:, ROT_DIM:]], axis=1)

    def silu(g):
        hg = 0.5 * g
        return hg + hg * jnp.tanh(hg)

    def put(ref, i, row0, val):
        for c in range(TSUB // LANE):
            ref[0, i * (TSUB // LANE) + c, row0:row0 + val.shape[0], :] = val[:, c * LANE:(c + 1) * LANE]

    def chunk(i, ht, kind, src, n, dst):
        r = jnp.dot(w_ref[src:src + n, :], ht, preferred_element_type=F32)
        if kind == "gate":
            put(sgT_ref, i, dst, silu(r).astype(BF16))
        elif kind == "v":
            put(vT_ref, i, dst, r.astype(BF16))
        elif kind == "qa":
            put(qT_ref, i, dst, rope(head_norm(r, qna_ref), i).reshape(n, TSUB).astype(BF16))
        elif kind == "qb":
            put(qT_ref, i, dst, head_norm(r, qnb_ref).reshape(n, TSUB).astype(BF16))
        else:
            kk = head_norm(r, kna_ref if kind == "ka" else knb_ref)
            kk = (rope(kk, i) if kind == "ka" else kk).reshape(n, TSUB)
            k_ref[0, i * TSUB:(i + 1) * TSUB, dst:dst + n] = kk.T.astype(BF16)

    half_a, half_b = A_WIDTH // 2, B_WIDTH // 2
    chunks = [
        ("gate", OFF_GA, half_a, 0), ("qa", OFF_QA, half_a, 0),
        ("gate", OFF_GA + half_a, half_a, half_a), ("qa", OFF_QA + half_a, half_a, half_a),
        ("gate", OFF_GB, half_b, A_WIDTH), ("qb", OFF_QB, half_b, A_WIDTH),
        ("gate", OFF_GB + half_b, half_b, A_WIDTH + half_b), ("qb", OFF_QB + half_b, half_b, A_WIDTH + half_b),
        ("ka", OFF_KA, A_KV_WIDTH, 0), ("v", OFF_VA, A_KV_WIDTH, 0),
        ("kb", OFF_KB, half_b, A_KV_WIDTH), ("kb", OFF_KB + half_b, half_b, A_KV_WIDTH + half_b),
        ("v", OFF_VB, half_b, A_KV_WIDTH), ("v", OFF_VB + half_b, half_b, A_KV_WIDTH + half_b),
    ]
    hts = [norm_ht(0)] + [None] * (n_sub - 1)
    for i in range(n_sub):
        for j, (kind, src, n, dst) in enumerate(chunks):
            chunk(i, hts[i], kind, src, n, dst)
            if j == 1 and i + 1 < n_sub:
                hts[i + 1] = norm_ht(i + 1)


def _project(x, nw, w_in_t, qna, kna, qnb, knb, cos_t, sin_t):
    b = x.shape[0]
    const = lambda *shape: pl.BlockSpec(shape, lambda i, j: (0,) * len(shape))
    blk = TM // LANE
    return pl.pallas_call(
        _proj_kernel,
        grid=(b, SEQ // TM),
        in_specs=[
            pl.BlockSpec((1, TM, D_MODEL), lambda i, j: (i, j, 0)),
            const(1, D_MODEL),
            const(IN_WIDTH, D_MODEL),
            const(HEAD_DIM, 1), const(HEAD_DIM, 1), const(HEAD_DIM, 1), const(HEAD_DIM, 1),
            pl.BlockSpec((ROT_HALF, TM), lambda i, j: (0, j)),
            pl.BlockSpec((ROT_HALF, TM), lambda i, j: (0, j)),
        ],
        out_specs=[
            pl.BlockSpec((1, blk, MIX_WIDTH, LANE), lambda i, j: (i, j, 0, 0)),
            pl.BlockSpec((1, blk, KV_WIDTH, LANE), lambda i, j: (i, j, 0, 0)),
            pl.BlockSpec((1, blk, MIX_WIDTH, LANE), lambda i, j: (i, j, 0, 0)),
            pl.BlockSpec((1, TM, KV_WIDTH), lambda i, j: (i, j, 0)),
        ],
        out_shape=[
            jax.ShapeDtypeStruct((b, N_BLK, MIX_WIDTH, LANE), BF16),
            jax.ShapeDtypeStruct((b, N_BLK, KV_WIDTH, LANE), BF16),
            jax.ShapeDtypeStruct((b, N_BLK, MIX_WIDTH, LANE), BF16),
            jax.ShapeDtypeStruct((b, SEQ, KV_WIDTH), BF16),
        ],
        compiler_params=pltpu.CompilerParams(
            dimension_semantics=("arbitrary", "arbitrary"), vmem_limit_bytes=VMEM_LIMIT_BYTES),
        name="proj",
    )(x, nw, w_in_t, qna, kna, qnb, knb, cos_t, sin_t)


def _attn_kernel(idx_ref, qT_ref, sgT_ref, k_ref, vT_ref, x_ref, p_ref, wout_ref, wgate_ref, wple_ref,
                 mask_ref, sink_ref, bias_ref, y_ref, oT_ref, s_ref, pr_ref, og_ref, y1_ref, y1b_ref):
    step = pl.program_id(0)
    t_idx = lax.rem(jnp.minimum(step, pl.num_programs(0) - 2), SEQ // TQ)
    cur = lax.rem(step, 2)
    prev = 1 - cur
    nblk = TQ // LANE

    @pl.when(step == 0)
    def _():
        oT_ref[1] = jnp.zeros(oT_ref.shape[1:], F32)
    zq = jnp.zeros((HEAD_DIM, LANE), BF16)
    ones_rows = jnp.ones((BF16_SUBLANES, LANE), BF16)
    sub = LANE // 8

    units = []
    for jj in range(nblk):
        a_units = [("A", jj, g, half) for g in range(A_KV_HEADS) for half in range(2)]
        b_units = [("B", jj, hp, 0) for hp in range(B_HEADS // 2)]
        for a_u, b_u in zip(a_units, b_units):
            units += [a_u, b_u]
    state = [None] * len(units)

    def prep(u):
        kind, jj, i0, i1 = units[u]
        jb = t_idx * nblk + jj
        st = {"kind": kind, "jj": jj, "sbuf": u % (S_AHEAD + 1), "pbuf": u % 2}
        if kind == "A":
            g, half = i0, i1
            h0 = A_GROUP * g + 2 * half
            qcat = jnp.concatenate([qT_ref[0, jj, h0 * HEAD_DIM:(h0 + 1) * HEAD_DIM, :],
                                    qT_ref[0, jj, (h0 + 1) * HEAD_DIM:(h0 + 2) * HEAD_DIM, :]], axis=1)
            zc = jnp.zeros_like(qcat)
            st["rhs"] = jnp.concatenate([qcat, zc] if g == 0 else [zc, qcat], axis=0)
            st["n"] = A_KEY_BLKS
            st["kb0"] = jnp.clip(jb - 1, 0, N_BLK - A_KEY_BLKS)
            st["klanes"] = slice(0, A_KV_WIDTH)
            st["vrows"] = slice(g * HEAD_DIM, (g + 1) * HEAD_DIM)
            st["variant"] = jnp.where(jb == 0, 0, jnp.where(jb == N_BLK - 1, 2, 1))
            st["sink"] = sink_ref[2 * g + half]
            st["m"] = jnp.broadcast_to(st["sink"], (8, 2 * LANE))
            st["out"] = [(h0 * HEAD_DIM, slice(0, HEAD_DIM), slice(0, LANE)),
                         ((h0 + 1) * HEAD_DIM, slice(0, HEAD_DIM), slice(LANE, 2 * LANE))]
        else:
            hp = i0
            lo = A_KV_WIDTH + hp * LANE
            qlo = A_WIDTH + hp * LANE
            q0 = qT_ref[0, jj, qlo:qlo + HEAD_DIM, :]
            q1 = qT_ref[0, jj, qlo + HEAD_DIM:qlo + LANE, :]
            st["rhs"] = jnp.concatenate([jnp.concatenate([q0, zq], axis=1),
                                         jnp.concatenate([zq, q1], axis=1)], axis=0)
            st["n"] = B_KEY_BLKS
            st["kb0"] = jnp.clip(jb - 2, 0, N_BLK - B_KEY_BLKS)
            st["klanes"] = slice(lo, lo + LANE)
            st["vrows"] = slice(lo, lo + LANE)
            st["hp"] = hp
            st["jb"] = jb
            st["m"] = jnp.full((8, 2 * LANE), NEG_INF, F32)
            st["out"] = [(qlo, slice(0, HEAD_DIM), slice(0, LANE)),
                         (qlo + HEAD_DIM, slice(HEAD_DIM, LANE), slice(LANE, 2 * LANE))]
        state[u] = st

    def s_chunk(u, c):
        st = state[u]
        if c == 0:
            rows = pl.ds(pl.multiple_of(st["kb0"] * LANE, LANE), st["n"] * LANE)
            st["t"] = jnp.dot(k_ref[0, rows, st["klanes"]], st["rhs"], preferred_element_type=F32)
        t = st["t"][c * LANE:(c + 1) * LANE, :]
        if st["kind"] == "A":
            mk = mask_ref[st["variant"], c]
            bias = jnp.concatenate([mk, mk], axis=1)
        else:
            e0 = idx_ref[st["jb"] * B_KEY_ROWS + 2 * c]
            e1 = idx_ref[st["jb"] * B_KEY_ROWS + 2 * c + 1]
            hp = st["hp"]
            bias = jnp.concatenate(
                [jnp.concatenate([bias_ref[e0, 2 * hp], bias_ref[e0, 2 * hp + 1]], axis=1),
                 jnp.concatenate([bias_ref[e1, 2 * hp], bias_ref[e1, 2 * hp + 1]], axis=1)], axis=0)
        t = t + bias
        s_ref[st["sbuf"], c * LANE:(c + 1) * LANE, :] = t
        st["m"] = jnp.maximum(st["m"], jnp.max(t.reshape(sub, 8, 2 * LANE), axis=0))

    def s_finish(u):
        st = state[u]
        st["m"] = jnp.broadcast_to(jnp.max(st["m"], axis=0, keepdims=True), (8, 2 * LANE))

    def e_chunk(u, c):
        st = state[u]
        t = s_ref[st["sbuf"], c * LANE:(c + 1) * LANE, :]
        d = (t.reshape(sub, 8, 2 * LANE) - st["m"][None]).reshape(LANE, 2 * LANE)
        pr_ref[st["pbuf"], c * LANE:(c + 1) * LANE, :] = jnp.exp2(d.astype(BF16))
        if c % PV_CHUNKS == PV_CHUNKS - 1 or c == st["n"] - 1:
            c0 = c - c % PV_CHUNKS
            v = jnp.concatenate(
                [jnp.concatenate([vT_ref[0, st["kb0"] + i, st["vrows"], :], ones_rows], axis=0)
                 for i in range(c0, c + 1)], axis=1)
            part = jnp.dot(v, pr_ref[st["pbuf"], c0 * LANE:(c + 1) * LANE, :], preferred_element_type=F32)
            st["o"] = part if c0 == 0 else st["o"] + part

    def pv(u):
        st = state[u]
        o = st["o"]
        nv = o.shape[0] - BF16_SUBLANES
        den = o[nv:nv + 1, :]
        if st["kind"] == "A":
            den = den + jnp.exp2(st["sink"] - st["m"][0:1, :])
        r = 1.0 / den
        lanes = slice(st["jj"] * LANE, (st["jj"] + 1) * LANE)
        for row0, rsl, lsl in st["out"]:
            oT_ref[cur, row0:row0 + HEAD_DIM, lanes] = o[rsl, lsl] * r[:, lsl]
        state[u] = None

    n_units = len(units)
    def out_transpose(jj):
        lanes = slice(jj * LANE, (jj + 1) * LANE)
        og = oT_ref[prev, :, lanes] * sgT_ref[0, jj].astype(F32)
        og_ref[lanes, :] = og.T.astype(BF16)

    def out_proj(rb, c):
        rows = slice(rb * PROJ_M, (rb + 1) * PROJ_M)
        cols = slice(c * PROJ_N, (c + 1) * PROJ_N)
        y1 = x_ref[0, rows, cols] + jnp.dot(og_ref[rows, :], wout_ref[:, cols], preferred_element_type=F32)
        y1_ref[rows, cols] = y1
        y1b_ref[rows, cols] = y1.astype(BF16)

    def gate_proj(rb, c):
        rows = slice(rb * PROJ_M, (rb + 1) * PROJ_M)
        cols = slice(c * PROJ_N, (c + 1) * PROJ_N)
        z = jnp.dot(y1b_ref[rows, :], wgate_ref[:, cols], preferred_element_type=F32)
        ple = jnp.dot(p_ref[0, rows, :].astype(BF16), wple_ref[:, cols], preferred_element_type=F32)
        hp = 0.5 * ple
        y_ref[0, rows, cols] = y1_ref[rows, cols] + (hp + hp * jnp.tanh(0.5 * z))

    n_proj = D_MODEL // PROJ_N
    pending = []
    for rb in range(TQ // PROJ_M):
        pending += [functools.partial(out_proj, rb, c) for c in range(n_proj)]
        pending += [functools.partial(gate_proj, rb, c) for c in range(n_proj)]

    for jj in range(nblk):
        out_transpose(jj)
    for v in range(min(S_AHEAD, n_units)):
        prep(v)
        for c in range(state[v]["n"]):
            s_chunk(v, c)
        s_finish(v)
        pending.pop(0)()
    n_pieces = len(pending)
    for u in range(n_units):
        nxt = u + S_AHEAD if u + S_AHEAD < n_units else None
        if nxt is not None:
            prep(nxt)
        n_s = state[nxt]["n"] if nxt is not None else 0
        n_e = state[u]["n"]
        for c in range(max(n_s, n_e)):
            if c < n_s:
                s_chunk(nxt, c)
            if c < n_e:
                e_chunk(u, c)
        if nxt is not None:
            s_finish(nxt)
        pv(u)
        while n_pieces - len(pending) < (u + 1) * n_pieces // n_units:
            pending.pop(0)()
    assert not pending


def _attend(idx, qT, sgT, k, vT, x, p, w_out, w_gate, w_ple, mask_a, sink_rows, bias_tab):
    b = x.shape[0]
    const = lambda *shape: pl.BlockSpec(shape, lambda s, idx_ref: (0,) * len(shape))
    blk = TQ // LANE
    nt = SEQ // TQ
    n_tiles = b * nt
    att = lambda s: jnp.minimum(s, n_tiles - 1)
    prj = lambda s: jnp.maximum(s - 1, 0)
    grid_spec = pltpu.PrefetchScalarGridSpec(
        num_scalar_prefetch=1,
        grid=(n_tiles + 1,),
        in_specs=[
            pl.BlockSpec((1, blk, MIX_WIDTH, LANE), lambda s, idx_ref: (att(s) // nt, att(s) % nt, 0, 0)),
            pl.BlockSpec((1, blk, MIX_WIDTH, LANE), lambda s, idx_ref: (prj(s) // nt, prj(s) % nt, 0, 0)),
            pl.BlockSpec((1, SEQ, KV_WIDTH), lambda s, idx_ref: (att(s) // nt, 0, 0)),
            pl.BlockSpec((1, N_BLK, KV_WIDTH, LANE), lambda s, idx_ref: (att(s) // nt, 0, 0, 0)),
            pl.BlockSpec((1, TQ, D_MODEL), lambda s, idx_ref: (prj(s) // nt, prj(s) % nt, 0)),
            pl.BlockSpec((1, TQ, PLE_DIM), lambda s, idx_ref: (prj(s) // nt, prj(s) % nt, 0)),
            const(MIX_WIDTH, D_MODEL),
            const(D_MODEL, D_MODEL),
            const(PLE_DIM, D_MODEL),
            const(*mask_a.shape),
            const(*sink_rows.shape),
            const(*bias_tab.shape),
        ],
        out_specs=pl.BlockSpec((1, TQ, D_MODEL), lambda s, idx_ref: (prj(s) // nt, prj(s) % nt, 0)),
        scratch_shapes=[
            pltpu.VMEM((2, MIX_WIDTH, TQ), F32),
            pltpu.VMEM((S_AHEAD + 1, B_KEY_BLKS * LANE, 2 * LANE), F32),
            pltpu.VMEM((2, B_KEY_BLKS * LANE, 2 * LANE), BF16),
            pltpu.VMEM((TQ, MIX_WIDTH), BF16),
            pltpu.VMEM((TQ, D_MODEL), F32),
            pltpu.VMEM((TQ, D_MODEL), BF16),
        ],
    )
    return pl.pallas_call(
        _attn_kernel,
        grid_spec=grid_spec,
        out_shape=jax.ShapeDtypeStruct((b, SEQ, D_MODEL), F32),
        compiler_params=pltpu.CompilerParams(
            dimension_semantics=("arbitrary",), vmem_limit_bytes=VMEM_LIMIT_BYTES),
        name="attn",
    )(idx, qT, sgT, k, vT, x, p, w_out, w_gate, w_ple, mask_a, sink_rows, bias_tab)


def _rope_tables():
    inv_freq = jnp.power(ROPE_THETA, -jnp.arange(0, ROT_DIM, 2, dtype=F32) / ROT_DIM)
    ang = jnp.arange(SEQ).astype(F32)[:, None] * inv_freq[None, :]
    return jnp.cos(ang).T, jnp.sin(ang).T


def _mask_a():
    i = np.arange(A_KEY_BLKS * LANE)[:, None]
    q = np.arange(LANE)[None, :]
    out = []
    for jb in (0, 1, N_BLK - 1):
        kb0 = min(max(jb - 1, 0), N_BLK - A_KEY_BLKS)
        diff = (kb0 * LANE + i) - (jb * LANE + q)
        out.append(np.where(np.abs(diff) <= WINDOW, 0.0, NEG_INF).reshape(A_KEY_BLKS, LANE, LANE))
    return np.stack(out).astype(np.float32)


def _bias_plan():
    entries = {}
    idx = np.zeros((GRID_ROWS // B_ROWS_PER_BLK, B_KEY_ROWS), np.int32)
    for rp in range(GRID_ROWS // B_ROWS_PER_BLK):
        r = B_ROWS_PER_BLK * rp
        kr0 = min(max(r - NA_ROWS // 2, 0), GRID_ROWS - B_KEY_ROWS)
        rs_l = min(max(r - NA_ROWS // 2, 0), GRID_ROWS - NA_ROWS)
        rs_r = min(max(r + 1 - NA_ROWS // 2, 0), GRID_ROWS - NA_ROWS)
        for a in range(B_KEY_ROWS):
            kr = kr0 + a
            v_l = rs_l <= kr < rs_l + NA_ROWS
            v_r = rs_r <= kr < rs_r + NA_ROWS
            d_l = kr - r + NA_ROWS - 1
            key = (bool(v_l), bool(v_r), d_l if (v_l or v_r) else 0)
            idx[rp, a] = entries.setdefault(key, len(entries))
    return list(entries.keys()), idx.reshape(-1)


def _bias_table(rpb, entries):
    ck = np.arange(GRID_W)[:, None]
    cq = np.arange(GRID_W)[None, :]
    cs = np.clip(cq - NA_COLS // 2, 0, GRID_W - NA_COLS)
    valid = (ck >= cs) & (ck < cs + NA_COLS)
    dcol = np.clip(ck - cq, -(NA_COLS - 1), NA_COLS - 1) + NA_COLS - 1
    onehot = (dcol.reshape(-1)[None, :] == np.arange(2 * NA_COLS - 1)[:, None]).astype(np.float32)
    sel = jnp.einsum("hdk,kc->hdc", rpb, jnp.asarray(onehot), precision=lax.Precision.HIGHEST)
    tab = jnp.where(valid[None, None], sel.reshape(B_HEADS, -1, GRID_W, GRID_W), NEG_INF)
    neg = jnp.full((B_HEADS, GRID_W, GRID_W), NEG_INF, F32)
    blocks = []
    for v_l, v_r, d_l in entries:
        left = tab[:, d_l] if v_l else neg
        right = tab[:, d_l - 1] if v_r else neg
        blocks.append(jnp.concatenate([left, right], axis=-1))
    return jnp.stack(blocks)


def _layer(x, p, consts):
    (nw, w_in_t, qna, kna, qnb, knb, cos_t, sin_t, idx, w_out, w_gate, w_ple, mask_a, sink_rows,
     bias_tab) = consts
    qT, vT, sgT, k = _project(x, nw, w_in_t, qna, kna, qnb, knb, cos_t, sin_t)
    return _attend(idx, qT, sgT, k, vT, x, p, w_out, w_gate, w_ple, mask_a, sink_rows, bias_tab)


def kernel(x_prompt, x_sample, p_prompt, p_sample, norm_w, w_in, q_norm_a, k_norm_a, sink_a,
           q_norm_b, k_norm_b, rpb_b, w_out, w_ple, w_ple_gate):
    assert x_prompt.shape[1:] == (SEQ, D_MODEL) and x_sample.shape[1:] == (SEQ, D_MODEL)
    assert norm_w.shape[0] == 1, "one layer"
    col = lambda v: v[0].reshape(HEAD_DIM, 1).astype(F32)
    cos_t, sin_t = _rope_tables()
    entries, idx = _bias_plan()
    consts = (
        norm_w[0].reshape(1, D_MODEL),
        w_in[0].T.astype(BF16),
        col(q_norm_a) * (SCALE * LOG2E), col(k_norm_a), col(q_norm_b) * (SCALE * LOG2E), col(k_norm_b),
        cos_t, sin_t,
        jnp.asarray(idx),
        w_out[0].astype(BF16), w_ple_gate[0].astype(BF16), w_ple[0].astype(BF16),
        jnp.asarray(_mask_a()),
        jnp.repeat(sink_a[0].astype(F32) * LOG2E, LANE).reshape(A_HEADS // 2, 1, 2 * LANE),
        _bias_table(rpb_b[0].astype(F32) * LOG2E, entries),
    )
    return (_layer(x_prompt, p_prompt[0], consts), _layer(x_sample, p_sample[0], consts))
```

```python
import functools

import numpy as np
import jax
import jax.numpy as jnp
from jax import lax
from jax.experimental import pallas as pl
from jax.experimental.pallas import tpu as pltpu

D_MODEL = 1024
SEQ = 2048
HEAD_DIM = 64
A_HEADS = 8
A_KV_HEADS = 2
A_GROUP = A_HEADS // A_KV_HEADS
B_HEADS = 8
A_WIDTH = A_HEADS * HEAD_DIM
A_KV_WIDTH = A_KV_HEADS * HEAD_DIM
B_WIDTH = B_HEADS * HEAD_DIM
MIX_WIDTH = A_WIDTH + B_WIDTH
KV_WIDTH = A_KV_WIDTH + B_WIDTH
WINDOW = 128
ROT_DIM = HEAD_DIM // 4
ROT_HALF = ROT_DIM // 2
ROPE_THETA = 500000.0
GRID_W = 64
GRID_ROWS = SEQ // GRID_W
NA_ROWS = 8
NA_COLS = 16
PLE_DIM = 256
EPS = 1e-6
SCALE = HEAD_DIM ** -0.5
LOG2E = float(np.log2(np.e))

OFF_QA = 0
OFF_KA = OFF_QA + A_WIDTH
OFF_VA = OFF_KA + A_KV_WIDTH
OFF_GA = OFF_VA + A_KV_WIDTH
OFF_QB = OFF_GA + A_WIDTH
OFF_KB = OFF_QB + B_WIDTH
OFF_VB = OFF_KB + B_WIDTH
OFF_GB = OFF_VB + B_WIDTH
IN_WIDTH = OFF_GB + B_WIDTH

LANE = 128
BF16_SUBLANES = 16
N_BLK = SEQ // LANE
A_KEY_BLKS = 3
B_ROWS_PER_BLK = LANE // GRID_W
B_KEY_ROWS = 10
B_KEY_BLKS = B_KEY_ROWS * GRID_W // LANE
TM = 1024
TSUB = 256
TQ = 512
PROJ_N = 256
PROJ_M = 256
PV_CHUNKS = 2
S_AHEAD = 2
VMEM_LIMIT_BYTES = 56 * 1024 * 1024

F32 = jnp.float32
BF16 = jnp.bfloat16
NEG_INF = float("-inf")


def _proj_kernel(x_ref, nw_ref, w_ref, qna_ref, kna_ref, qnb_ref, knb_ref, cos_ref, sin_ref,
                 qT_ref, vT_ref, sgT_ref, k_ref):
    n_sub = x_ref.shape[1] // TSUB

    def norm_ht(i):
        x = x_ref[0, i * TSUB:(i + 1) * TSUB, :]
        ms = jnp.mean(x * x, axis=-1, keepdims=True)
        return (x * lax.rsqrt(ms + EPS) * nw_ref[...]).T.astype(BF16)

    def head_norm(r, g_ref):
        r3 = r.reshape(r.shape[0] // HEAD_DIM, HEAD_DIM, r.shape[1])
        m2 = jnp.mean(r3 * r3, axis=1, keepdims=True)
        return r3 * lax.rsqrt(m2 + EPS) * g_ref[...][None]

    def rope(y, i):
        c = cos_ref[:, i * TSUB:(i + 1) * TSUB][None]
        s = sin_ref[:, i * TSUB:(i + 1) * TSUB][None]
        x1 = y[:, 0:ROT_HALF]
        x2 = y[:, ROT_HALF:ROT_DIM]
        return jnp.concatenate([x1 * c - x2 * s, x2 * c + x1 * s, y[:, ROT_DIM:]], axis=1)

    def silu(g):
        hg = 0.5 * g
        return hg + hg * jnp.tanh(hg)

    def put(ref, i, row0, val):
        for c in range(TSUB // LANE):
            ref[0, i * (TSUB // LANE) + c, row0:row0 + val.shape[0], :] = val[:, c * LANE:(c + 1) * LANE]

    def chunk(i, ht, kind, src, n, dst):
        r = jnp.dot(w_ref[src:src + n, :], ht, preferred_element_type=F32)
        if kind == "gate":
            put(sgT_ref, i, dst, silu(r).astype(BF16))
        elif kind == "v":
            put(vT_ref, i, dst, r.astype(BF16))
        elif kind == "qa":
            put(qT_ref, i, dst, rope(head_norm(r, qna_ref), i).reshape(n, TSUB).astype(BF16))
        elif kind == "qb":
            put(qT_ref, i, dst, head_norm(r, qnb_ref).reshape(n, TSUB).astype(BF16))
        else:
            kk = head_norm(r, kna_ref if kind == "ka" else knb_ref)
            kk = (rope(kk, i) if kind == "ka" else kk).reshape(n, TSUB)
            k_ref[0, i * TSUB:(i + 1) * TSUB, dst:dst + n] = kk.T.astype(BF16)

    half_a, half_b = A_WIDTH // 2, B_WIDTH // 2
    chunks = [
        ("gate", OFF_GA, half_a, 0), ("qa", OFF_QA, half_a, 0),
        ("gate", OFF_GA + half_a, half_a, half_a), ("qa", OFF_QA + half_a, half_a, half_a),
        ("gate", OFF_GB, half_b, A_WIDTH), ("qb", OFF_QB, half_b, A_WIDTH),
        ("gate", OFF_GB + half_b, half_b, A_WIDTH + half_b), ("qb", OFF_QB + half_b, half_b, A_WIDTH + half_b),
        ("ka", OFF_KA, A_KV_WIDTH, 0), ("v", OFF_VA, A_KV_WIDTH, 0),
        ("kb", OFF_KB, half_b, A_KV_WIDTH), ("kb", OFF_KB + half_b, half_b, A_KV_WIDTH + half_b),
        ("v", OFF_VB, half_b, A_KV_WIDTH), ("v", OFF_VB + half_b, half_b, A_KV_WIDTH + half_b),
    ]
    v_first = [ch for ch in chunks if ch[0] == "v"] + [ch for ch in chunks if ch[0] != "v"]
    hts = [norm_ht(0)] + [None] * (n_sub - 1)
    for i in range(n_sub):
        for j, (kind, src, n, dst) in enumerate(v_first if i == 0 else chunks):
            chunk(i, hts[i], kind, src, n, dst)
            if j == 1 and i + 1 < n_sub:
                hts[i + 1] = norm_ht(i + 1)


def _project(x, nw, w_in_t, qna, kna, qnb, knb, cos_t, sin_t):
    b = x.shape[0]
    const = lambda *shape: pl.BlockSpec(shape, lambda i, j: (0,) * len(shape))
    blk = TM // LANE
    return pl.pallas_call(
        _proj_kernel,
        grid=(b, SEQ // TM),
        in_specs=[
            pl.BlockSpec((1, TM, D_MODEL), lambda i, j: (i, j, 0)),
            const(1, D_MODEL),
            const(IN_WIDTH, D_MODEL),
            const(HEAD_DIM, 1), const(HEAD_DIM, 1), const(HEAD_DIM, 1), const(HEAD_DIM, 1),
            pl.BlockSpec((ROT_HALF, TM), lambda i, j: (0, j)),
            pl.BlockSpec((ROT_HALF, TM), lambda i, j: (0, j)),
        ],
        out_specs=[
            pl.BlockSpec((1, blk, MIX_WIDTH, LANE), lambda i, j: (i, j, 0, 0)),
            pl.BlockSpec((1, blk, KV_WIDTH, LANE), lambda i, j: (i, j, 0, 0)),
            pl.BlockSpec((1, blk, MIX_WIDTH, LANE), lambda i, j: (i, j, 0, 0)),
            pl.BlockSpec((1, TM, KV_WIDTH), lambda i, j: (i, j, 0)),
        ],
        out_shape=[
            jax.ShapeDtypeStruct((b, N_BLK, MIX_WIDTH, LANE), BF16),
            jax.ShapeDtypeStruct((b, N_BLK, KV_WIDTH, LANE), BF16),
            jax.ShapeDtypeStruct((b, N_BLK, MIX_WIDTH, LANE), BF16),
            jax.ShapeDtypeStruct((b, SEQ, KV_WIDTH), BF16),
        ],
        compiler_params=pltpu.CompilerParams(
            dimension_semantics=("arbitrary", "arbitrary"), vmem_limit_bytes=VMEM_LIMIT_BYTES),
        name="proj",
    )(x, nw, w_in_t, qna, kna, qnb, knb, cos_t, sin_t)


def _attn_kernel(idx_ref, qT_ref, sgT_ref, k_ref, vT_ref, x_ref, p_ref, wout_ref, wgate_ref, wple_ref,
                 mask_ref, sink_ref, bias_ref, y_ref, oT_ref, s_ref, pr_ref, og_ref, y1_ref, y1b_ref):
    step = pl.program_id(0)
    t_idx = lax.rem(jnp.minimum(step, pl.num_programs(0) - 2), SEQ // TQ)
    cur = lax.rem(step, 2)
    prev = 1 - cur
    nblk = TQ // LANE

    @pl.when(step == 0)
    def _():
        oT_ref[1] = jnp.zeros(oT_ref.shape[1:], F32)
    zq = jnp.zeros((HEAD_DIM, LANE), BF16)
    ones_rows = jnp.ones((BF16_SUBLANES, LANE), BF16)
    sub = LANE // 8

    units = []
    for jj in range(nblk):
        a_units = [("A", jj, g, half) for g in range(A_KV_HEADS) for half in range(2)]
        b_units = [("B", jj, hp, 0) for hp in range(B_HEADS // 2)]
        for a_u, b_u in zip(a_units, b_units):
            units += [a_u, b_u]
    state = [None] * len(units)

    def prep(u):
        kind, jj, i0, i1 = units[u]
        jb = t_idx * nblk + jj
        st = {"kind": kind, "jj": jj, "sbuf": u % (S_AHEAD + 1), "pbuf": u % 2}
        if kind == "A":
            g, half = i0, i1
            h0 = A_GROUP * g + 2 * half
            qcat = jnp.concatenate([qT_ref[0, jj, h0 * HEAD_DIM:(h0 + 1) * HEAD_DIM, :],
                                    qT_ref[0, jj, (h0 + 1) * HEAD_DIM:(h0 + 2) * HEAD_DIM, :]], axis=1)
            zc = jnp.zeros_like(qcat)
            st["rhs"] = jnp.concatenate([qcat, zc] if g == 0 else [zc, qcat], axis=0)
            st["n"] = A_KEY_BLKS
            st["kb0"] = jnp.clip(jb - 1, 0, N_BLK - A_KEY_BLKS)
            st["klanes"] = slice(0, A_KV_WIDTH)
            st["vrows"] = slice(g * HEAD_DIM, (g + 1) * HEAD_DIM)
            st["variant"] = jnp.where(jb == 0, 0, jnp.where(jb == N_BLK - 1, 2, 1))
            st["sink"] = sink_ref[2 * g + half]
            st["m"] = jnp.broadcast_to(st["sink"], (8, 2 * LANE))
            st["out"] = [(h0 * HEAD_DIM, slice(0, HEAD_DIM), slice(0, LANE)),
                         ((h0 + 1) * HEAD_DIM, slice(0, HEAD_DIM), slice(LANE, 2 * LANE))]
        else:
            hp = i0
            lo = A_KV_WIDTH + hp * LANE
            qlo = A_WIDTH + hp * LANE
            q0 = qT_ref[0, jj, qlo:qlo + HEAD_DIM, :]
            q1 = qT_ref[0, jj, qlo + HEAD_DIM:qlo + LANE, :]
            st["rhs"] = jnp.concatenate([jnp.concatenate([q0, zq], axis=1),
                                         jnp.concatenate([zq, q1], axis=1)], axis=0)
            st["n"] = B_KEY_BLKS
            st["kb0"] = jnp.clip(jb - 2, 0, N_BLK - B_KEY_BLKS)
            st["klanes"] = slice(lo, lo + LANE)
            st["vrows"] = slice(lo, lo + LANE)
            st["hp"] = hp
            st["jb"] = jb
            st["m"] = jnp.full((8, 2 * LANE), NEG_INF, F32)
            st["out"] = [(qlo, slice(0, HEAD_DIM), slice(0, LANE)),
                         (qlo + HEAD_DIM, slice(HEAD_DIM, LANE), slice(LANE, 2 * LANE))]
        state[u] = st

    def s_chunk(u, c):
        st = state[u]
        if c == 0:
            rows = pl.ds(pl.multiple_of(st["kb0"] * LANE, LANE), st["n"] * LANE)
            st["t"] = jnp.dot(k_ref[0, rows, st["klanes"]], st["rhs"], preferred_element_type=F32)
        t = st["t"][c * LANE:(c + 1) * LANE, :]
        if st["kind"] == "A":
            mk = mask_ref[st["variant"], c]
            bias = jnp.concatenate([mk, mk], axis=1)
        else:
            e0 = idx_ref[st["jb"] * B_KEY_ROWS + 2 * c]
            e1 = idx_ref[st["jb"] * B_KEY_ROWS + 2 * c + 1]
            hp = st["hp"]
            bias = jnp.concatenate(
                [jnp.concatenate([bias_ref[e0, 2 * hp], bias_ref[e0, 2 * hp + 1]], axis=1),
                 jnp.concatenate([bias_ref[e1, 2 * hp], bias_ref[e1, 2 * hp + 1]], axis=1)], axis=0)
        t = t + bias
        s_ref[st["sbuf"], c * LANE:(c + 1) * LANE, :] = t
        st["m"] = jnp.maximum(st["m"], jnp.max(t.reshape(sub, 8, 2 * LANE), axis=0))

    def s_finish(u):
        st = state[u]
        st["m"] = jnp.broadcast_to(jnp.max(st["m"], axis=0, keepdims=True), (8, 2 * LANE))

    def e_chunk(u, c):
        st = state[u]
        t = s_ref[st["sbuf"], c * LANE:(c + 1) * LANE, :]
        d = (t.reshape(sub, 8, 2 * LANE) - st["m"][None]).reshape(LANE, 2 * LANE)
        pr_ref[st["pbuf"], c * LANE:(c + 1) * LANE, :] = jnp.exp2(d.astype(BF16))
        if c % PV_CHUNKS == PV_CHUNKS - 1 or c == st["n"] - 1:
            c0 = c - c % PV_CHUNKS
            v = jnp.concatenate(
                [jnp.concatenate([vT_ref[0, st["kb0"] + i, st["vrows"], :], ones_rows], axis=0)
                 for i in range(c0, c + 1)], axis=1)
            part = jnp.dot(v, pr_ref[st["pbuf"], c0 * LANE:(c + 1) * LANE, :], preferred_element_type=F32)
            st["o"] = part if c0 == 0 else st["o"] + part

    def pv(u):
        st = state[u]
        o = st["o"]
        nv = o.shape[0] - BF16_SUBLANES
        den = o[nv:nv + 1, :]
        if st["kind"] == "A":
            den = den + jnp.exp2(st["sink"] - st["m"][0:1, :])
        r = 1.0 / den
        lanes = slice(st["jj"] * LANE, (st["jj"] + 1) * LANE)
        for row0, rsl, lsl in st["out"]:
            oT_ref[cur, row0:row0 + HEAD_DIM, lanes] = o[rsl, lsl] * r[:, lsl]
        state[u] = None

    n_units = len(units)
    def out_transpose(jj):
        lanes = slice(jj * LANE, (jj + 1) * LANE)
        og = oT_ref[prev, :, lanes] * sgT_ref[0, jj].astype(F32)
        og_ref[lanes, :] = og.T.astype(BF16)

    def out_proj(rb, c):
        rows = slice(rb * PROJ_M, (rb + 1) * PROJ_M)
        cols = slice(c * PROJ_N, (c + 1) * PROJ_N)
        y1 = x_ref[0, rows, cols] + jnp.dot(og_ref[rows, :], wout_ref[:, cols], preferred_element_type=F32)
        y1_ref[rows, cols] = y1
        y1b_ref[rows, cols] = y1.astype(BF16)

    def gate_proj(rb, c):
        rows = slice(rb * PROJ_M, (rb + 1) * PROJ_M)
        cols = slice(c * PROJ_N, (c + 1) * PROJ_N)
        z = jnp.dot(y1b_ref[rows, :], wgate_ref[:, cols], preferred_element_type=F32)
        ple = jnp.dot(p_ref[0, rows, :].astype(BF16), wple_ref[:, cols], preferred_element_type=F32)
        hp = 0.5 * ple
        y_ref[0, rows, cols] = y1_ref[rows, cols] + (hp + hp * jnp.tanh(0.5 * z))

    n_proj = D_MODEL // PROJ_N
    pending = []
    for rb in range(TQ // PROJ_M):
        pending += [functools.partial(out_proj, rb, c) for c in range(n_proj)]
        pending += [functools.partial(gate_proj, rb, c) for c in range(n_proj)]

    for jj in range(nblk):
        out_transpose(jj)
    for v in range(min(S_AHEAD, n_units)):
        prep(v)
        for c in range(state[v]["n"]):
            s_chunk(v, c)
        s_finish(v)
        pending.pop(0)()
    n_pieces = len(pending)
    for u in range(n_units):
        nxt = u + S_AHEAD if u + S_AHEAD < n_units else None
        if nxt is not None:
            prep(nxt)
        n_s = state[nxt]["n"] if nxt is not None else 0
        n_e = state[u]["n"]
        for c in range(max(n_s, n_e)):
            if c < n_s:
                s_chunk(nxt, c)
            if c < n_e:
                e_chunk(u, c)
        if nxt is not None:
            s_finish(nxt)
        pv(u)
        while n_pieces - len(pending) < (u + 1) * n_pieces // n_units:
            pending.pop(0)()
    assert not pending


def _attend(idx, qT, sgT, k, vT, x, p, w_out, w_gate, w_ple, mask_a, sink_rows, bias_tab):
    b = x.shape[0]
    const = lambda *shape: pl.BlockSpec(shape, lambda s, idx_ref: (0,) * len(shape))
    blk = TQ // LANE
    nt = SEQ // TQ
    n_tiles = b * nt
    att = lambda s: jnp.minimum(s, n_tiles - 1)
    prj = lambda s: jnp.maximum(s - 1, 0)
    grid_spec = pltpu.PrefetchScalarGridSpec(
        num_scalar_prefetch=1,
        grid=(n_tiles + 1,),
        in_specs=[
            pl.BlockSpec((1, blk, MIX_WIDTH, LANE), lambda s, idx_ref: (att(s) // nt, att(s) % nt, 0, 0)),
            pl.BlockSpec((1, blk, MIX_WIDTH, LANE), lambda s, idx_ref: (prj(s) // nt, prj(s) % nt, 0, 0)),
            pl.BlockSpec((1, SEQ, KV_WIDTH), lambda s, idx_ref: (att(s) // nt, 0, 0)),
            pl.BlockSpec((1, N_BLK, KV_WIDTH, LANE), lambda s, idx_ref: (att(s) // nt, 0, 0, 0)),
            pl.BlockSpec((1, TQ, D_MODEL), lambda s, idx_ref: (prj(s) // nt, prj(s) % nt, 0)),
            pl.BlockSpec((1, TQ, PLE_DIM), lambda s, idx_ref: (prj(s) // nt, prj(s) % nt, 0)),
            const(MIX_WIDTH, D_MODEL),
            const(D_MODEL, D_MODEL),
            const(PLE_DIM, D_MODEL),
            const(*mask_a.shape),
            const(*sink_rows.shape),
            const(*bias_tab.shape),
        ],
        out_specs=pl.BlockSpec((1, TQ, D_MODEL), lambda s, idx_ref: (prj(s) // nt, prj(s) % nt, 0)),
        scratch_shapes=[
            pltpu.VMEM((2, MIX_WIDTH, TQ), F32),
            pltpu.VMEM((S_AHEAD + 1, B_KEY_BLKS * LANE, 2 * LANE), F32),
            pltpu.VMEM((2, B_KEY_BLKS * LANE, 2 * LANE), BF16),
            pltpu.VMEM((TQ, MIX_WIDTH), BF16),
            pltpu.VMEM((TQ, D_MODEL), F32),
            pltpu.VMEM((TQ, D_MODEL), BF16),
        ],
    )
    return pl.pallas_call(
        _attn_kernel,
        grid_spec=grid_spec,
        out_shape=jax.ShapeDtypeStruct((b, SEQ, D_MODEL), F32),
        compiler_params=pltpu.CompilerParams(
            dimension_semantics=("arbitrary",), vmem_limit_bytes=VMEM_LIMIT_BYTES),
        name="attn",
    )(idx, qT, sgT, k, vT, x, p, w_out, w_gate, w_ple, mask_a, sink_rows, bias_tab)


def _rope_tables():
    inv_freq = jnp.power(ROPE_THETA, -jnp.arange(0, ROT_DIM, 2, dtype=F32) / ROT_DIM)
    ang = jnp.arange(SEQ).astype(F32)[:, None] * inv_freq[None, :]
    return jnp.cos(ang).T, jnp.sin(ang).T


def _mask_a():
    i = np.arange(A_KEY_BLKS * LANE)[:, None]
    q = np.arange(LANE)[None, :]
    out = []
    for jb in (0, 1, N_BLK - 1):
        kb0 = min(max(jb - 1, 0), N_BLK - A_KEY_BLKS)
        diff = (kb0 * LANE + i) - (jb * LANE + q)
        out.append(np.where(np.abs(diff) <= WINDOW, 0.0, NEG_INF).reshape(A_KEY_BLKS, LANE, LANE))
    return np.stack(out).astype(np.float32)


def _bias_plan():
    entries = {}
    idx = np.zeros((GRID_ROWS // B_ROWS_PER_BLK, B_KEY_ROWS), np.int32)
    for rp in range(GRID_ROWS // B_ROWS_PER_BLK):
        r = B_ROWS_PER_BLK * rp
        kr0 = min(max(r - NA_ROWS // 2, 0), GRID_ROWS - B_KEY_ROWS)
        rs_l = min(max(r - NA_ROWS // 2, 0), GRID_ROWS - NA_ROWS)
        rs_r = min(max(r + 1 - NA_ROWS // 2, 0), GRID_ROWS - NA_ROWS)
        for a in range(B_KEY_ROWS):
            kr = kr0 + a
            v_l = rs_l <= kr < rs_l + NA_ROWS
            v_r = rs_r <= kr < rs_r + NA_ROWS
            d_l = kr - r + NA_ROWS - 1
            key = (bool(v_l), bool(v_r), d_l if (v_l or v_r) else 0)
            idx[rp, a] = entries.setdefault(key, len(entries))
    return list(entries.keys()), idx.reshape(-1)


def _bias_table(rpb, entries):
    ck = np.arange(GRID_W)[:, None]
    cq = np.arange(GRID_W)[None, :]
    cs = np.clip(cq - NA_COLS // 2, 0, GRID_W - NA_COLS)
    valid = (ck >= cs) & (ck < cs + NA_COLS)
    dcol = np.clip(ck - cq, -(NA_COLS - 1), NA_COLS - 1) + NA_COLS - 1
    onehot = (dcol.reshape(-1)[None, :] == np.arange(2 * NA_COLS - 1)[:, None]).astype(np.float32)
    sel = jnp.einsum("hdk,kc->hdc", rpb, jnp.asarray(onehot), precision=lax.Precision.HIGHEST)
    tab = jnp.where(valid[None, None], sel.reshape(B_HEADS, -1, GRID_W, GRID_W), NEG_INF)
    neg = jnp.full((B_HEADS, GRID_W, GRID_W), NEG_INF, F32)
    blocks = []
    for v_l, v_r, d_l in entries:
        left = tab[:, d_l] if v_l else neg
        right = tab[:, d_l - 1] if v_r else neg
        blocks.append(jnp.concatenate([left, right], axis=-1))
    return jnp.stack(blocks)


def _layer(x, p, consts):
    (nw, w_in_t, qna, kna, qnb, knb, cos_t, sin_t, idx, w_out, w_gate, w_ple, mask_a, sink_rows,
     bias_tab) = consts
    qT, vT, sgT, k = _project(x, nw, w_in_t, qna, kna, qnb, knb, cos_t, sin_t)
    return _attend(idx, qT, sgT, k, vT, x, p, w_out, w_gate, w_ple, mask_a, sink_rows, bias_tab)


def kernel(x_prompt, x_sample, p_prompt, p_sample, norm_w, w_in, q_norm_a, k_norm_a, sink_a,
           q_norm_b, k_norm_b, rpb_b, w_out, w_ple, w_ple_gate):
    assert x_prompt.shape[1:] == (SEQ, D_MODEL) and x_sample.shape[1:] == (SEQ, D_MODEL)
    assert norm_w.shape[0] == 1, "one layer"
    col = lambda v: v[0].reshape(HEAD_DIM, 1).astype(F32)
    cos_t, sin_t = _rope_tables()
    entries, idx = _bias_plan()
    consts = (
        norm_w[0].reshape(1, D_MODEL),
        w_in[0].T.astype(BF16),
        col(q_norm_a) * (SCALE * LOG2E), col(k_norm_a), col(q_norm_b) * (SCALE * LOG2E), col(k_norm_b),
        cos_t, sin_t,
        jnp.asarray(idx),
        w_out[0].astype(BF16), w_ple_gate[0].astype(BF16), w_ple[0].astype(BF16),
        jnp.asarray(_mask_a()),
        jnp.repeat(sink_a[0].astype(F32) * LOG2E, LANE).reshape(A_HEADS // 2, 1, 2 * LANE),
        _bias_table(rpb_b[0].astype(F32) * LOG2E, entries),
    )
    return (_layer(x_prompt, p_prompt[0], consts), _layer(x_sample, p_sample[0], consts))
```
